```python
import math
import jax
import jax.numpy as jnp
from jax import lax
import numpy as np

D_MODEL = 1024
BATCH = 8
SEQ = 4096
DEPTH = 1
DEC_BATCH = 32
DEC_SEQ = 8
PAST_LEN = 16384
PAGE_SIZE = 128

SSM_WIDTH = D_MODEL // 2
SSM_GROUP = 16
SSM_GROUPS = SSM_WIDTH // SSM_GROUP
SSM_STATE = 64
MIN_DECAY = 1e-4
N_HEADS = 4
HEAD_DIM = 64
ATTN_QK = N_HEADS * 2 * HEAD_DIM
ATTN_V = N_HEADS * 2 * HEAD_DIM
ROT_DIM = HEAD_DIM // 4
ROPE_THETA = 500000.0
Q_BLOCK = 128
NEG_INF = -1e30
N_EXPERTS = 32
TOP_K = 4
D_FF = D_MODEL
SWIGLU_LIMIT = 7.0
SWIGLU_ALPHA = 1.702
MOE_BLOCK = 128
EPS = 1e-6
SPLITS = (SSM_WIDTH, SSM_WIDTH + ATTN_QK, SSM_WIDTH + 2 * ATTN_QK, SSM_WIDTH + 2 * ATTN_QK + ATTN_V, SSM_WIDTH + 2 * ATTN_QK + ATTN_V + D_MODEL)
IN_COLS = SSM_WIDTH + 2 * ATTN_QK + ATTN_V + 2 * D_MODEL

kernel_name = 'hybrid_s5_diffattn_moe_adaln_step'


def rms_norm(x, g):
    xf = x.astype(jnp.float32)
    y = xf * lax.rsqrt(jnp.mean(xf * xf, axis=-1, keepdims=True) + EPS)
    return (y * g.astype(jnp.float32)).astype(x.dtype)


def lambda_init(layer_idx):
    return 0.8 - 0.6 * math.exp(-0.3 * layer_idx)


def rope_partial(x, pos):
    half = ROT_DIM // 2
    inv_freq = ROPE_THETA ** (-jnp.arange(half, dtype=jnp.float32) * (2.0 / ROT_DIM))
    ang = pos.astype(jnp.float32)[:, None] * inv_freq[None, :]
    cos = jnp.cos(ang)[:, None, :]
    sin = jnp.sin(ang)[:, None, :]
    xr = x[..., :ROT_DIM].astype(jnp.float32)
    x1, x2 = xr[..., :half], xr[..., half:]
    rot = jnp.concatenate([x1 * cos - x2 * sin, x2 * cos + x1 * sin], axis=-1)
    return jnp.concatenate([rot.astype(x.dtype), x[..., ROT_DIM:]], axis=-1)


def _complex_affine_combine(first, second):
    ar1, ai1, br1, bi1 = first
    ar2, ai2, br2, bi2 = second
    return (ar2 * ar1 - ai2 * ai1,
            ar2 * ai1 + ai2 * ar1,
            ar2 * br1 - ai2 * bi1 + br2,
            ar2 * bi1 + ai2 * br1 + bi2)


def ssm_branch(u, h0_re, h0_im, a_re, a_im, log_dt, b_re, b_im, c_re, c_im, d_skip, w_glu, b_glu):
    bsz, seq = u.shape[0], u.shape[1]
    ug = u.astype(jnp.float32).reshape(bsz, seq, SSM_GROUPS, SSM_GROUP)
    lam_re = jnp.minimum(a_re.astype(jnp.float32), -MIN_DECAY)
    lam_im = a_im.astype(jnp.float32)
    dt = jnp.exp(log_dt.astype(jnp.float32))[:, None]
    mag = jnp.exp(lam_re * dt)
    ab_re = mag * jnp.cos(lam_im * dt)
    ab_im = mag * jnp.sin(lam_im * dt)
    den = lam_re * lam_re + lam_im * lam_im
    n_re = ab_re - 1.0
    n_im = ab_im
    zoh_re = (n_re * lam_re + n_im * lam_im) / den
    zoh_im = (n_im * lam_re - n_re * lam_im) / den
    br = b_re.astype(jnp.float32)
    bi = b_im.astype(jnp.float32)
    bb_re = zoh_re[..., None] * br - zoh_im[..., None] * bi
    bb_im = zoh_re[..., None] * bi + zoh_im[..., None] * br
    bu_re = jnp.einsum('blgc,gpc->lbgp', ug, bb_re)
    bu_im = jnp.einsum('blgc,gpc->lbgp', ug, bb_im)
    e_re = jnp.concatenate([h0_re.astype(jnp.float32)[None], bu_re], axis=0)
    e_im = jnp.concatenate([h0_im.astype(jnp.float32)[None], bu_im], axis=0)
    a_r = jnp.broadcast_to(ab_re[None, None], (seq + 1, 1, SSM_GROUPS, SSM_STATE))
    a_i = jnp.broadcast_to(ab_im[None, None], (seq + 1, 1, SSM_GROUPS, SSM_STATE))
    _, _, h_re, h_im = lax.associative_scan(_complex_affine_combine, (a_r, a_i, e_re, e_im), axis=0)
    h_re = h_re[1:]
    h_im = h_im[1:]
    y = (jnp.einsum('lbgp,gcp->blgc', h_re, c_re.astype(jnp.float32))
         - jnp.einsum('lbgp,gcp->blgc', h_im, c_im.astype(jnp.float32))
         + d_skip.astype(jnp.float32) * ug)
    y = jax.nn.gelu(y.reshape(bsz, seq, SSM_WIDTH))
    y = y * jax.nn.sigmoid(y @ w_glu.astype(jnp.float32) + b_glu.astype(jnp.float32))
    return y.astype(u.dtype), h_re[-1], h_im[-1]


def diff_attend(q, k, v, q_pos, k_pos, lam):
    s = jnp.einsum('bqhcd,bkhcd->bhcqk', q.astype(jnp.float32), k.astype(jnp.float32)) * (HEAD_DIM ** -0.5)
    causal = k_pos[None, :] <= q_pos[:, None]
    p = jax.nn.softmax(jnp.where(causal, s, NEG_INF), axis=-1)
    attn = p[:, :, 0] - lam * p[:, :, 1]
    return jnp.einsum('bhqk,bkhe->bqhe', attn, v.astype(jnp.float32))


def diff_attention(q, k_all, v_all, q_pos, lam, subln_g, lam_init):
    bsz, n_q = q.shape[0], q.shape[1]
    n_k = k_all.shape[1]
    qh = q.reshape(bsz, n_q, N_HEADS, 2, HEAD_DIM)
    kh = k_all.reshape(bsz, n_k, N_HEADS, 2, HEAD_DIM)
    k_pos = jnp.arange(n_k)
    if n_q > Q_BLOCK and n_q % Q_BLOCK == 0:
        n_blk = n_q // Q_BLOCK
        qb = jnp.moveaxis(qh.reshape(bsz, n_blk, Q_BLOCK, N_HEADS, 2, HEAD_DIM), 1, 0)
        pb = q_pos.reshape(n_blk, Q_BLOCK)
        ob = lax.map(lambda blk: diff_attend(blk[0], kh, v_all, blk[1], k_pos, lam), (qb, pb))
        out = jnp.moveaxis(ob, 0, 1).reshape(bsz, n_q, N_HEADS, 2 * HEAD_DIM)
    else:
        out = diff_attend(qh, kh, v_all, q_pos, k_pos, lam)
    out = rms_norm(out, subln_g) * (1.0 - lam_init)
    return out.reshape(bsz, n_q, ATTN_V).astype(q.dtype)


def routed_ffn(h, w_router, b_router, w_up, b_up, w_down, b_down):
    n_tok = h.shape[0]
    logits = h.astype(jnp.float32) @ w_router.astype(jnp.float32) + b_router.astype(jnp.float32)
    top_v, top_i = lax.top_k(logits, TOP_K)
    gates = jax.nn.softmax(top_v, axis=-1)
    n_assign = n_tok * TOP_K
    e_flat = top_i.reshape(n_assign)
    t_flat = jnp.repeat(jnp.arange(n_tok, dtype=jnp.int32), TOP_K)
    g_flat = gates.reshape(n_assign)
    order = jnp.argsort(e_flat)
    e_sorted = e_flat[order]
    counts = jnp.bincount(e_flat, length=N_EXPERTS)
    padded = ((counts + MOE_BLOCK - 1) // MOE_BLOCK) * MOE_BLOCK
    start = jnp.cumsum(counts) - counts
    pend = jnp.cumsum(padded)
    pstart = pend - padded
    dest = pstart[e_sorted] + (jnp.arange(n_assign) - start[e_sorted])
    n_rows = ((n_assign + MOE_BLOCK - 1) // MOE_BLOCK) * MOE_BLOCK + N_EXPERTS * MOE_BLOCK
    n_blocks = n_rows // MOE_BLOCK
    row_tok = jnp.zeros((n_rows,), jnp.int32).at[dest].set(t_flat[order])
    row_gate = jnp.zeros((n_rows,), jnp.float32).at[dest].set(g_flat[order])
    blk_exp = jnp.minimum(jnp.searchsorted(pend, jnp.arange(n_blocks) * MOE_BLOCK, side='right'), N_EXPERTS - 1)

    def expert_block(args):
        toks, e = args
        xb = h[toks]
        gu = xb @ w_up[e] + b_up[e]
        g_lin = jnp.minimum(gu[:, :D_FF], SWIGLU_LIMIT)
        u_lin = jnp.clip(gu[:, D_FF:], -SWIGLU_LIMIT, SWIGLU_LIMIT)
        act = g_lin * jax.nn.sigmoid(SWIGLU_ALPHA * g_lin) * (u_lin + 1.0)
        return act @ w_down[e] + b_down[e]

    out = lax.map(expert_block, (row_tok.reshape(n_blocks, MOE_BLOCK), blk_exp))
    out = out.reshape(n_rows, D_MODEL).astype(jnp.float32) * row_gate[:, None]
    return jax.ops.segment_sum(out, row_tok, num_segments=n_tok).astype(h.dtype)


def trunk_layer(x, c, pos, k_past, v_past, h0_re, h0_im,
                norm1_g, w_ada, b_ada, w_in, ssm_a_re, ssm_a_im, ssm_log_dt, ssm_b_re, ssm_b_im,
                ssm_c_re, ssm_c_im, ssm_d, w_glu, b_glu, lambda_q1, lambda_k1, lambda_q2, lambda_k2,
                subln_g, p_ssm, p_attn, w_o, norm2_g, w_router, b_router, w_up, b_up, w_down, b_down,
                lam_init):
    bsz, seq = x.shape[0], x.shape[1]
    mod = jax.nn.silu(c) @ w_ada + b_ada
    shift1, scale1, gate1, shift2, scale2, gate2 = jnp.split(mod[:, None, :], 6, axis=-1)
    h = rms_norm(x, norm1_g) * (1.0 + scale1) + shift1
    proj = h @ w_in
    u, q, k, v, g_ssm, g_attn = jnp.split(proj, SPLITS, axis=-1)
    q = rope_partial(q.reshape(bsz, seq, 2 * N_HEADS, HEAD_DIM), pos)
    k = rope_partial(k.reshape(bsz, seq, 2 * N_HEADS, HEAD_DIM), pos)
    v = v.reshape(bsz, seq, N_HEADS, 2 * HEAD_DIM)
    if k_past is None:
        k_all, v_all = k, v
    else:
        k_all = jnp.concatenate([k_past.astype(k.dtype), k], axis=1)
        v_all = jnp.concatenate([v_past.astype(v.dtype), v], axis=1)
    lam = (jnp.exp(jnp.sum(lambda_q1.astype(jnp.float32) * lambda_k1.astype(jnp.float32)))
           - jnp.exp(jnp.sum(lambda_q2.astype(jnp.float32) * lambda_k2.astype(jnp.float32)))
           + lam_init)
    y_attn = diff_attention(q, k_all, v_all, pos, lam, subln_g, lam_init)
    y_ssm, hT_re, hT_im = ssm_branch(u, h0_re, h0_im, ssm_a_re, ssm_a_im, ssm_log_dt, ssm_b_re, ssm_b_im,
                                     ssm_c_re, ssm_c_im, ssm_d, w_glu, b_glu)
    merged = jax.nn.sigmoid(g_ssm) * (y_ssm @ p_ssm) + jax.nn.sigmoid(g_attn) * (y_attn @ p_attn)
    x = x + gate1 * (merged @ w_o)
    h = rms_norm(x, norm2_g) * (1.0 + scale2) + shift2
    ffn = routed_ffn(h.reshape(bsz * seq, D_MODEL), w_router, b_router, w_up, b_up, w_down, b_down)
    x = x + gate2 * ffn.reshape(bsz, seq, D_MODEL)
    return x, k, v, hT_re, hT_im


def setup_inputs(seed: int = 0) -> dict:
    key = jax.random.key(seed)
    keys = iter(jax.random.split(key, 48))

    def nrm(shape, scale):
        return jax.random.normal(next(keys), shape, jnp.float32) * scale

    n_pages = PAST_LEN // PAGE_SIZE
    n_used = DEC_BATCH * n_pages
    n_pool = (n_used * 5) // 4
    L, G, P, C, E = DEPTH, SSM_GROUPS, SSM_STATE, SSM_GROUP, N_EXPERTS
    x_prompt = nrm((BATCH, SEQ, D_MODEL), 1.0)
    x_sample = nrm((DEC_BATCH, DEC_SEQ, D_MODEL), 1.0)
    c_prompt = nrm((BATCH, D_MODEL), 1.0)
    c_sample = nrm((DEC_BATCH, D_MODEL), 1.0)
    cache_k = nrm((L, n_pool, PAGE_SIZE, 2 * N_HEADS, HEAD_DIM), 1.0)
    cache_v = nrm((L, n_pool, PAGE_SIZE, N_HEADS, 2 * HEAD_DIM), 1.0)
    state_ssm_re = nrm((L, DEC_BATCH, G, P), 0.3)
    state_ssm_im = nrm((L, DEC_BATCH, G, P), 0.3)
    perm = jax.random.permutation(next(keys), n_pool)
    page_table = perm[:n_used].reshape(DEC_BATCH, n_pages).astype(jnp.int32)
    norm1_g = 1.0 + nrm((L, D_MODEL), 0.02)
    w_ada = nrm((L, D_MODEL, 6 * D_MODEL), D_MODEL ** -0.5)
    b_ada = nrm((L, 6 * D_MODEL), 0.02)
    w_in = nrm((L, D_MODEL, IN_COLS), D_MODEL ** -0.5)
    ssm_a_re = -0.5 + nrm((L, G, P), 0.01)
    ssm_a_im = jnp.pi * jnp.arange(P, dtype=jnp.float32)[None, None, :] + nrm((L, G, P), 0.01)
    ssm_log_dt = jax.random.uniform(next(keys), (L, G), jnp.float32, math.log(1e-3), math.log(1e-1))
    ssm_b_re = nrm((L, G, P, C), (2.0 * C) ** -0.5)
    ssm_b_im = nrm((L, G, P, C), (2.0 * C) ** -0.5)
    ssm_c_re = nrm((L, G, C, P), (0.5 * P) ** -0.5)
    ssm_c_im = nrm((L, G, C, P), (0.5 * P) ** -0.5)
    ssm_d = nrm((L, G, C), 1.0)
    w_glu = nrm((L, SSM_WIDTH, SSM_WIDTH), SSM_WIDTH ** -0.5)
    b_glu = nrm((L, SSM_WIDTH), 0.02)
    lambda_q1 = nrm((L, HEAD_DIM), 0.1)
    lambda_k1 = nrm((L, HEAD_DIM), 0.1)
    lambda_q2 = nrm((L, HEAD_DIM), 0.1)
    lambda_k2 = nrm((L, HEAD_DIM), 0.1)
    subln_g = 1.0 + nrm((L, 2 * HEAD_DIM), 0.02)
    p_ssm = nrm((L, SSM_WIDTH, D_MODEL), SSM_WIDTH ** -0.5)
    p_attn = nrm((L, ATTN_V, D_MODEL), ATTN_V ** -0.5)
    w_o = nrm((L, D_MODEL, D_MODEL), D_MODEL ** -0.5)
    norm2_g = 1.0 + nrm((L, D_MODEL), 0.02)
    w_router = nrm((L, D_MODEL, E), D_MODEL ** -0.5)
    b_router = nrm((L, E), 0.01)
    w_up = nrm((L, E, D_MODEL, 2 * D_FF), D_MODEL ** -0.5)
    b_up = nrm((L, E, 2 * D_FF), 0.02)
    w_down = nrm((L, E, D_FF, D_MODEL), D_FF ** -0.5)
    b_down = nrm((L, E, D_MODEL), 0.02)
    final_g = 1.0 + nrm((D_MODEL,), 0.02)
    return {'x_prompt': x_prompt, 'x_sample': x_sample, 'c_prompt': c_prompt, 'c_sample': c_sample,
            'cache_k': cache_k, 'cache_v': cache_v, 'state_ssm_re': state_ssm_re, 'state_ssm_im': state_ssm_im,
            'page_table': page_table, 'norm1_g': norm1_g, 'w_ada': w_ada, 'b_ada': b_ada, 'w_in': w_in,
            'ssm_a_re': ssm_a_re, 'ssm_a_im': ssm_a_im, 'ssm_log_dt': ssm_log_dt, 'ssm_b_re': ssm_b_re,
            'ssm_b_im': ssm_b_im, 'ssm_c_re': ssm_c_re, 'ssm_c_im': ssm_c_im, 'ssm_d': ssm_d,
            'w_glu': w_glu, 'b_glu': b_glu, 'lambda_q1': lambda_q1, 'lambda_k1': lambda_k1,
            'lambda_q2': lambda_q2, 'lambda_k2': lambda_k2, 'subln_g': subln_g, 'p_ssm': p_ssm,
            'p_attn': p_attn, 'w_o': w_o, 'norm2_g': norm2_g, 'w_router': w_router, 'b_router': b_router,
            'w_up': w_up, 'b_up': b_up, 'w_down': w_down, 'b_down': b_down, 'final_g': final_g}


def reference(x_prompt, x_sample, c_prompt, c_sample, cache_k, cache_v, state_ssm_re, state_ssm_im,
              page_table, norm1_g, w_ada, b_ada, w_in, ssm_a_re, ssm_a_im, ssm_log_dt, ssm_b_re, ssm_b_im,
              ssm_c_re, ssm_c_im, ssm_d, w_glu, b_glu, lambda_q1, lambda_k1, lambda_q2, lambda_k2, subln_g,
              p_ssm, p_attn, w_o, norm2_g, w_router, b_router, w_up, b_up, w_down, b_down, final_g):
    bsz_p, seq_p = x_prompt.shape[0], x_prompt.shape[1]
    bsz_s, seq_s = x_sample.shape[0], x_sample.shape[1]
    n_pages = page_table.shape[1]
    past_len = n_pages * cache_k.shape[2]
    pos_p = jnp.arange(seq_p)
    pos_s = past_len + jnp.arange(seq_s)
    xp, xs = x_prompt, x_sample
    kp_rows, vp_rows, rp_state, ip_state = [], [], [], []
    ks_rows, vs_rows, rs_state, is_state = [], [], [], []
    for l in range(DEPTH):
        weights = (norm1_g[l], w_ada[l], b_ada[l], w_in[l], ssm_a_re[l], ssm_a_im[l], ssm_log_dt[l],
                   ssm_b_re[l], ssm_b_im[l], ssm_c_re[l], ssm_c_im[l], ssm_d[l], w_glu[l], b_glu[l],
                   lambda_q1[l], lambda_k1[l], lambda_q2[l], lambda_k2[l], subln_g[l], p_ssm[l], p_attn[l],
                   w_o[l], norm2_g[l], w_router[l], b_router[l], w_up[l], b_up[l], w_down[l], b_down[l],
                   lambda_init(l))
        zeros_state = jnp.zeros((bsz_p, SSM_GROUPS, SSM_STATE), jnp.float32)
        xp, kp, vp, rp, ip = trunk_layer(xp, c_prompt, pos_p, None, None, zeros_state, zeros_state, *weights)
        k_past = cache_k[l][page_table].reshape(bsz_s, past_len, 2 * N_HEADS, HEAD_DIM)
        v_past = cache_v[l][page_table].reshape(bsz_s, past_len, N_HEADS, 2 * HEAD_DIM)
        xs, ks, vs, rs, is_ = trunk_layer(xs, c_sample, pos_s, k_past, v_past,
                                          state_ssm_re[l], state_ssm_im[l], *weights)
        kp_rows.append(kp)
        vp_rows.append(vp)
        rp_state.append(rp)
        ip_state.append(ip)
        ks_rows.append(ks)
        vs_rows.append(vs)
        rs_state.append(rs)
        is_state.append(is_)
    y_prompt = rms_norm(xp, final_g)
    y_sample = rms_norm(xs, final_g)
    return (y_prompt, y_sample, jnp.stack(kp_rows), jnp.stack(vp_rows), jnp.stack(rp_state), jnp.stack(ip_state),
            jnp.stack(ks_rows), jnp.stack(vs_rows), jnp.stack(rs_state), jnp.stack(is_state))
```

```python
import functools
import math

import jax
import jax.numpy as jnp
from jax import lax
from jax.experimental import pallas as pl
from jax.experimental.pallas import tpu as pltpu

D_MODEL = 1024
SSM_WIDTH = D_MODEL // 2
SSM_GROUP = 16
SSM_GROUPS = SSM_WIDTH // SSM_GROUP
SSM_STATE = 64
SSM_LANES = SSM_GROUPS * SSM_STATE
MIN_DECAY = 1e-4
N_HEADS = 4
HEAD_DIM = 64
HEAD_W = 2 * HEAD_DIM
ATTN_W = N_HEADS * HEAD_W
ROT_DIM = HEAD_DIM // 4
ROT_HALF = ROT_DIM // 2
ROPE_THETA = 500000.0
NEG_INF = -1e30
N_EXPERTS = 32
TOP_K = 4
D_FF = D_MODEL
SWIGLU_LIMIT = 7.0
SWIGLU_ALPHA = 1.702
EPS = 1e-6
LAM_INIT = 0.8 - 0.6 * math.exp(-0.3 * 0)

VMEM_LIMIT_BYTES = 56 * 1024 * 1024
LANE = 128

BF16 = jnp.bfloat16
F32 = jnp.float32


def _cparams(sem):
    return pltpu.CompilerParams(dimension_semantics=sem, vmem_limit_bytes=VMEM_LIMIT_BYTES)


def _dot(a, b):
    return jnp.dot(a, b, preferred_element_type=F32)


def _rms(x, g):
    return x * lax.rsqrt(jnp.mean(x * x, axis=-1, keepdims=True) + EPS) * g


def _mod_kernel(c_ref, w_ref, b_ref, o_ref):
    c = c_ref[...]
    s = (c * jax.nn.sigmoid(c)).astype(BF16)
    o_ref[...] = _dot(s, w_ref[...].astype(BF16)) + b_ref[...]


def _modulation(c, w_ada, b_ada):
    n = c.shape[0]
    return pl.pallas_call(
        _mod_kernel,
        grid=(6,),
        in_specs=[pl.BlockSpec((n, D_MODEL), lambda j: (0, 0)),
                  pl.BlockSpec((D_MODEL, D_MODEL), lambda j: (0, j)),
                  pl.BlockSpec((1, D_MODEL), lambda j: (0, j))],
        out_specs=pl.BlockSpec((n, D_MODEL), lambda j: (0, j)),
        out_shape=jax.ShapeDtypeStruct((n, 6 * D_MODEL), F32),
        compiler_params=_cparams(("arbitrary",)),
        name="adaln_mod",
    )(c, w_ada, b_ada.reshape(1, 6 * D_MODEL))


def _inproj_kernel(x_ref, shift_ref, scale_ref, g_ref, wu_ref, wq_ref, wkt_ref, wv_ref, wg_ref,
                   cosq_ref, sina_ref, sinb_ref, cost_ref, sint_ref,
                   u_ref, q_ref, kt_ref, v_ref, gate_ref):
    x = x_ref[...]
    h = _rms(x, g_ref[...]) * (1.0 + scale_ref[...]) + shift_ref[...]
    hb = h.astype(BF16)
    u_ref[...] = _dot(hb, wu_ref[...])
    v_ref[...] = _dot(hb, wv_ref[...])
    gate_ref[...] = _dot(hb, wg_ref[...])
    q = _dot(hb, wq_ref[...])
    blocks = [q[:, j:j + LANE] for j in range(0, ATTN_W, LANE)]
    q_up = jnp.concatenate([pltpu.roll(b, LANE - ROT_HALF, 1) for b in blocks], axis=1)
    q_dn = jnp.concatenate([pltpu.roll(b, ROT_HALF, 1) for b in blocks], axis=1)
    q = q * cosq_ref[...] + q_up * sina_ref[...] + q_dn * sinb_ref[...]
    q_ref[...] = (q * (HEAD_DIM ** -0.5)).astype(BF16)
    kt = lax.dot_general(wkt_ref[...], hb, (((1,), (1,)), ((), ())), preferred_element_type=F32)
    cos_t = cost_ref[...]
    sin_t = sint_ref[...]
    for hh in range(2 * N_HEADS):
        r0 = hh * HEAD_DIM
        x1 = kt[r0:r0 + ROT_HALF, :]
        x2 = kt[r0 + ROT_HALF:r0 + ROT_DIM, :]
        kt_ref[r0:r0 + ROT_HALF, :] = x1 * cos_t - x2 * sin_t
        kt_ref[r0 + ROT_HALF:r0 + ROT_DIM, :] = x2 * cos_t + x1 * sin_t
        kt_ref[r0 + ROT_DIM:r0 + HEAD_DIM, :] = kt[r0 + ROT_DIM:r0 + HEAD_DIM, :]


def _rope_tables(pos):
    inv_freq = ROPE_THETA ** (-jnp.arange(ROT_HALF, dtype=F32) * (2.0 / ROT_DIM))
    ang = pos.astype(F32)[:, None] * inv_freq[None, :]
    cos, sin = jnp.cos(ang), jnp.sin(ang)
    s = pos.shape[0]
    pad = HEAD_DIM - ROT_DIM
    cos_h = jnp.concatenate([cos, cos, jnp.ones((s, pad), F32)], axis=1)
    sina_h = jnp.concatenate([-sin, jnp.zeros((s, ROT_HALF + pad), F32)], axis=1)
    sinb_h = jnp.concatenate([jnp.zeros((s, ROT_HALF), F32), sin, jnp.zeros((s, pad), F32)], axis=1)
    reps = ATTN_W // HEAD_DIM
    return (jnp.tile(cos_h, (1, reps)), jnp.tile(sina_h, (1, reps)), jnp.tile(sinb_h, (1, reps)),
            cos.T, sin.T)


def _in_projection(x, shift, scale, g, weights, tables, ts, per_token_mod):
    nb, s, _ = x.shape
    ns = s // ts
    wu, wq, wkt, wv, wg = weights
    cosq, sina, sinb, cost, sint = tables
    if per_token_mod:
        mod_spec = pl.BlockSpec((ts, D_MODEL), lambda b, i: (b * ns + i, 0))
    else:
        mod_spec = pl.BlockSpec((None, 1, D_MODEL), lambda b, i: (b, 0, 0))
    const = lambda shape: pl.BlockSpec(shape, lambda b, i: (0,) * len(shape))
    tab = pl.BlockSpec((ts, ATTN_W), lambda b, i: (i, 0))
    tab_t = pl.BlockSpec((ROT_HALF, ts), lambda b, i: (0, i))
    return pl.pallas_call(
        _inproj_kernel,
        grid=(nb, ns),
        in_specs=[pl.BlockSpec((None, ts, D_MODEL), lambda b, i: (b, i, 0)), mod_spec, mod_spec,
                  const((1, D_MODEL)),
                  const((D_MODEL, SSM_WIDTH)), const((D_MODEL, ATTN_W)), const((ATTN_W, D_MODEL)),
                  const((D_MODEL, ATTN_W)), const((D_MODEL, 2 * D_MODEL)),
                  tab, tab, tab, tab_t, tab_t],
        out_specs=[pl.BlockSpec((ts, SSM_WIDTH), lambda b, i: (i, b)),
                   pl.BlockSpec((None, ts, ATTN_W), lambda b, i: (b, i, 0)),
                   pl.BlockSpec((None, ATTN_W, ts), lambda b, i: (b, 0, i)),
                   pl.BlockSpec((None, ts, ATTN_W), lambda b, i: (b, i, 0)),
                   pl.BlockSpec((None, ts, 2 * D_MODEL), lambda b, i: (b, i, 0))],
        out_shape=[jax.ShapeDtypeStruct((s, nb * SSM_WIDTH), F32),
                   jax.ShapeDtypeStruct((nb, s, ATTN_W), BF16),
                   jax.ShapeDtypeStruct((nb, ATTN_W, s), F32),
                   jax.ShapeDtypeStruct((nb, s, ATTN_W), F32),
                   jax.ShapeDtypeStruct((nb, s, 2 * D_MODEL), F32)],
        compiler_params=_cparams(("arbitrary", "arbitrary")),
        name="in_proj",
    )(x, shift, scale, g, wu, wq, wkt, wv, wg, cosq, sina, sinb, cost, sint)


def _online_update(s, v_bf, m_ref, l_ref, acc_ref, c):
    m_old = m_ref[c]
    m_new = jnp.maximum(m_old, jnp.max(s, axis=-1, keepdims=True))
    p = jnp.exp(s - m_new)
    alpha = jnp.exp(m_old - m_new)
    l_ref[c] = alpha * l_ref[c] + jnp.sum(p, axis=-1, keepdims=True)
    acc_ref[c] = alpha * acc_ref[c] + _dot(p.astype(BF16), v_bf)
    m_ref[c] = m_new


def _diff_finalize(lam, l_ref, acc_ref, g):
    o = acc_ref[0] / l_ref[0] - lam * (acc_ref[1] / l_ref[1])
    return _rms(o, g) * (1.0 - LAM_INIT)


def _init_softmax_state(m_ref, l_ref, acc_ref):
    m_ref[...] = jnp.full(m_ref.shape, NEG_INF, F32)
    l_ref[...] = jnp.zeros(l_ref.shape, F32)
    acc_ref[...] = jnp.zeros(acc_ref.shape, F32)


def _prompt_attn_kernel(lam_ref, q_ref, kt_ref, v_ref, g_ref, o_ref, m_ref, l_ref, acc_ref, *, tq, tk):
    qi = pl.program_id(2)
    ki = pl.program_id(3)

    @pl.when(ki == 0)
    def _():
        _init_softmax_state(m_ref, l_ref, acc_ref)

    @pl.when(ki <= qi)
    def _():
        q = q_ref[...]
        v_bf = v_ref[...].astype(BF16)
        row = qi * tq + lax.broadcasted_iota(jnp.int32, (tq, tk), 0)
        col = ki * tk + lax.broadcasted_iota(jnp.int32, (tq, tk), 1)
        visible = col <= row
        for c in range(2):
            kt_c = kt_ref[c * HEAD_DIM:(c + 1) * HEAD_DIM, :].astype(BF16)
            s = _dot(q[:, c * HEAD_DIM:(c + 1) * HEAD_DIM], kt_c)
            s = jnp.where(visible, s, NEG_INF)
            _online_update(s, v_bf, m_ref, l_ref, acc_ref, c)

    @pl.when(ki == qi)
    def _():
        o_ref[...] = _diff_finalize(lam_ref[0], l_ref, acc_ref, g_ref[...])


def _prompt_attention(lam, q, kt, v, subln_g, tq):
    b, s, _ = q.shape
    tk = tq
    nq = s // tq
    kern = functools.partial(_prompt_attn_kernel, tq=tq, tk=tk)
    grid_spec = pltpu.PrefetchScalarGridSpec(
        num_scalar_prefetch=1,
        grid=(b, N_HEADS, nq, nq),
        in_specs=[pl.BlockSpec((None, tq, HEAD_W), lambda bi, h, qi, ki, lam: (bi, qi, h)),
                  pl.BlockSpec((None, HEAD_W, tk), lambda bi, h, qi, ki, lam: (bi, h, jnp.minimum(ki, qi))),
                  pl.BlockSpec((None, tk, HEAD_W), lambda bi, h, qi, ki, lam: (bi, jnp.minimum(ki, qi), h)),
                  pl.BlockSpec((1, HEAD_W), lambda bi, h, qi, ki, lam: (0, 0))],
        out_specs=pl.BlockSpec((None, tq, HEAD_W), lambda bi, h, qi, ki, lam: (bi, qi, h)),
        scratch_shapes=[pltpu.VMEM((2, tq, 1), F32), pltpu.VMEM((2, tq, 1), F32),
                        pltpu.VMEM((2, tq, HEAD_W), F32)])
    return pl.pallas_call(
        kern, grid_spec=grid_spec,
        out_shape=jax.ShapeDtypeStruct((b, s, ATTN_W), F32),
        compiler_params=_cparams(("arbitrary",) * 4),
        name="prompt_attn",
    )(lam, q, kt, v, subln_g)


def _sample_attn_kernel(pt_ref, lam_ref, q_ref, ktn_ref, vn_ref, g_ref, *rest, pages, page):
    k_refs = rest[:pages]
    v_refs = rest[pages:2 * pages]
    o_ref, kcat_ref, vcat_ref, m_ref, l_ref, acc_ref = rest[2 * pages:]
    j = pl.program_id(1)
    nj = pl.num_programs(1)
    nq = q_ref.shape[0]

    @pl.when(j == 0)
    def _():
        _init_softmax_state(m_ref, l_ref, acc_ref)

    for i in range(pages):
        kcat_ref[:, i * page:(i + 1) * page] = k_refs[i][...].astype(BF16)
        for h in range(N_HEADS):
            vcat_ref[h, i * page:(i + 1) * page, :] = v_refs[i][pl.ds(h, page, stride=N_HEADS), :].astype(BF16)

    q = q_ref[...]
    for h in range(N_HEADS):
        v_bf = vcat_ref[h]
        for c in range(2):
            r0 = (2 * h + c) * HEAD_DIM
            s = _dot(q[:, r0:r0 + HEAD_DIM], kcat_ref[r0:r0 + HEAD_DIM, :])
            _online_update(s, v_bf, m_ref, l_ref, acc_ref, 2 * h + c)

    @pl.when(j == nj - 1)
    def _():
        row = lax.broadcasted_iota(jnp.int32, (nq, nq), 0)
        col = lax.broadcasted_iota(jnp.int32, (nq, nq), 1)
        visible = col <= row
        lam = lam_ref[0]
        g = g_ref[...]
        for h in range(N_HEADS):
            v_bf = vn_ref[:, h * HEAD_W:(h + 1) * HEAD_W].astype(BF16)
            for c in range(2):
                r0 = (2 * h + c) * HEAD_DIM
                s = _dot(q[:, r0:r0 + HEAD_DIM], ktn_ref[r0:r0 + HEAD_DIM, :].astype(BF16))
                s = jnp.where(visible, s, NEG_INF)
                _online_update(s, v_bf, m_ref, l_ref, acc_ref, 2 * h + c)
            o = acc_ref[2 * h] / l_ref[2 * h] - lam * (acc_ref[2 * h + 1] / l_ref[2 * h + 1])
            o_ref[:, h * HEAD_W:(h + 1) * HEAD_W] = _rms(o, g) * (1.0 - LAM_INIT)


def _sample_attention(page_table, lam, q, kt_new, v_new, subln_g, cache_kt, cache_v2, pages):
    b, nq, _ = q.shape
    n_pages = page_table.shape[1]
    page = cache_kt.shape[2]
    nj = n_pages // pages
    kern = functools.partial(_sample_attn_kernel, pages=pages, page=page)

    def k_spec(i):
        return pl.BlockSpec((None, ATTN_W, page), lambda bi, j, pt, lam: (pt[bi, j * pages + i], 0, 0))

    def v_spec(i):
        return pl.BlockSpec((None, page * N_HEADS, HEAD_W), lambda bi, j, pt, lam: (pt[bi, j * pages + i], 0, 0))

    grid_spec = pltpu.PrefetchScalarGridSpec(
        num_scalar_prefetch=2,
        grid=(b, nj),
        in_specs=[pl.BlockSpec((None, nq, ATTN_W), lambda bi, j, pt, lam: (bi, 0, 0)),
                  pl.BlockSpec((None, ATTN_W, nq), lambda bi, j, pt, lam: (bi, 0, 0)),
                  pl.BlockSpec((None, nq, ATTN_W), lambda bi, j, pt, lam: (bi, 0, 0)),
                  pl.BlockSpec((1, HEAD_W), lambda bi, j, pt, lam: (0, 0))]
                 + [k_spec(i) for i in range(pages)] + [v_spec(i) for i in range(pages)],
        out_specs=pl.BlockSpec((None, nq, ATTN_W), lambda bi, j, pt, lam: (bi, 0, 0)),
        scratch_shapes=[pltpu.VMEM((ATTN_W, pages * page), BF16),
                        pltpu.VMEM((N_HEADS, pages * page, HEAD_W), BF16),
                        pltpu.VMEM((2 * N_HEADS, nq, 1), F32), pltpu.VMEM((2 * N_HEADS, nq, 1), F32),
                        pltpu.VMEM((2 * N_HEADS, nq, HEAD_W), F32)])
    return pl.pallas_call(
        kern, grid_spec=grid_spec,
        out_shape=jax.ShapeDtypeStruct((b, nq, ATTN_W), F32),
        compiler_params=_cparams(("arbitrary", "arbitrary")),
        name="sample_attn",
    )(page_table, lam, q, kt_new, v_new, subln_g, *([cache_kt] * pages), *([cache_v2] * pages))


def _ssm_kernel(u_ref, h0r_ref, h0i_ref, ar_ref, ai_ref, bbr_ref, bbi_ref, ccr_ref, cci_ref, d_ref,
                wglu_ref, bglu_ref, y_ref, hr_out, hi_out, sr_ref, si_ref, br_ref, bi_ref,
                *, steps, nb, lane_chunk):
    i = pl.program_id(0)

    @pl.when(i == 0)
    def _():
        sr_ref[...] = h0r_ref[...]
        si_ref[...] = h0i_ref[...]

    u = u_ref[...]
    ub = u.astype(BF16)
    br_ref[...] = _dot(ub, bbr_ref[...])
    bi_ref[...] = _dot(ub, bbi_ref[...])

    for c0 in range(0, SSM_LANES, lane_chunk):
        lanes = pl.ds(c0, lane_chunk)
        a_r = jnp.broadcast_to(ar_ref[:, lanes], (nb, lane_chunk))
        a_i = jnp.broadcast_to(ai_ref[:, lanes], (nb, lane_chunk))

        def step(t, carry):
            h_r, h_i = carry
            rows = pl.ds(pl.multiple_of(t * nb, nb), nb)
            n_r = a_r * h_r - a_i * h_i + br_ref[rows, lanes]
            n_i = a_r * h_i + a_i * h_r + bi_ref[rows, lanes]
            br_ref[rows, lanes] = n_r
            bi_ref[rows, lanes] = n_i
            return n_r, n_i

        h_r, h_i = lax.fori_loop(0, steps, step, (sr_ref[:, lanes], si_ref[:, lanes]), unroll=4)
        sr_ref[:, lanes] = h_r
        si_ref[:, lanes] = h_i

    y = (_dot(br_ref[...].astype(BF16), ccr_ref[...]) - _dot(bi_ref[...].astype(BF16), cci_ref[...])
         + d_ref[...] * u)
    y = jax.nn.gelu(y, approximate=True)
    y_ref[...] = y * jax.nn.sigmoid(_dot(y.astype(BF16), wglu_ref[...]) + bglu_ref[...])

    @pl.when(i == pl.num_programs(0) - 1)
    def _():
        hr_out[...] = sr_ref[...]
        hi_out[...] = si_ref[...]


def _ssm_params(a_re, a_im, log_dt, b_re, b_im, c_re, c_im, d_skip):
    lam_re = jnp.minimum(a_re.astype(F32), -MIN_DECAY)
    lam_im = a_im.astype(F32)
    dt = jnp.exp(log_dt.astype(F32))[:, None]
    mag = jnp.exp(lam_re * dt)
    ab_re = mag * jnp.cos(lam_im * dt)
    ab_im = mag * jnp.sin(lam_im * dt)
    den = lam_re * lam_re + lam_im * lam_im
    n_re = ab_re - 1.0
    n_im = ab_im
    zoh_re = (n_re * lam_re + n_im * lam_im) / den
    zoh_im = (n_im * lam_re - n_re * lam_im) / den
    br, bi = b_re.astype(F32), b_im.astype(F32)
    bb_re = zoh_re[..., None] * br - zoh_im[..., None] * bi
    bb_im = zoh_re[..., None] * bi + zoh_im[..., None] * br
    eye = jnp.eye(SSM_GROUPS, dtype=F32)

    def in_blockdiag(bb):
        m = jnp.einsum('gpc,gh->gchp', bb, eye)
        return m.reshape(SSM_WIDTH, SSM_LANES).astype(BF16)

    def out_blockdiag(cc):
        m = jnp.einsum('gcp,gh->gphc', cc.astype(F32), eye)
        return m.reshape(SSM_LANES, SSM_WIDTH).astype(BF16)

    return (ab_re.reshape(1, SSM_LANES), ab_im.reshape(1, SSM_LANES), in_blockdiag(bb_re), in_blockdiag(bb_im),
            out_blockdiag(c_re), out_blockdiag(c_im), d_skip.astype(F32).reshape(1, SSM_WIDTH))


def _ssm_scan(u_tm, h0_re, h0_im, params, w_glu, b_glu, nb, steps):
    rows_total = u_tm.shape[0]
    rows = steps * nb
    n_chunks = rows_total // rows
    ab_re, ab_im, bbr, bbi, ccr, cci, d_row = params
    lane_chunk = min(SSM_LANES, 8 * 1024 // nb)
    kern = functools.partial(_ssm_kernel, steps=steps, nb=nb, lane_chunk=lane_chunk)
    const = lambda shape: pl.BlockSpec(shape, lambda i: (0,) * len(shape))
    return pl.pallas_call(
        kern, grid=(n_chunks,),
        in_specs=[pl.BlockSpec((rows, SSM_WIDTH), lambda i: (i, 0)),
                  const((nb, SSM_LANES)), const((nb, SSM_LANES)),
                  const((1, SSM_LANES)), const((1, SSM_LANES)),
                  const((SSM_WIDTH, SSM_LANES)), const((SSM_WIDTH, SSM_LANES)),
                  const((SSM_LANES, SSM_WIDTH)), const((SSM_LANES, SSM_WIDTH)),
                  const((1, SSM_WIDTH)), const((SSM_WIDTH, SSM_WIDTH)), const((1, SSM_WIDTH))],
        out_specs=[pl.BlockSpec((rows, SSM_WIDTH), lambda i: (i, 0)),
                   const((nb, SSM_LANES)), const((nb, SSM_LANES))],
        out_shape=[jax.ShapeDtypeStruct((rows_total, SSM_WIDTH), F32),
                   jax.ShapeDtypeStruct((nb, SSM_LANES), F32), jax.ShapeDtypeStruct((nb, SSM_LANES), F32)],
        scratch_shapes=[pltpu.VMEM((nb, SSM_LANES), F32), pltpu.VMEM((nb, SSM_LANES), F32),
                        pltpu.VMEM((rows, SSM_LANES), F32), pltpu.VMEM((rows, SSM_LANES), F32)],
        compiler_params=_cparams(("arbitrary",)),
        name="ssm_scan",
    )(u_tm, h0_re, h0_im, ab_re, ab_im, bbr, bbi, ccr, cci, d_row, w_glu, b_glu)


def _post_kernel(yssm_ref, yattn_ref, gate_ref, x_ref, gate1_ref, shift2_ref, scale2_ref,
                 pssm_ref, pattn_ref, wo_ref, g2_ref, wr_ref, br_ref, x1_ref, h2_ref, logit_ref):
    gates = gate_ref[...]
    a = _dot(yssm_ref[...].astype(BF16), pssm_ref[...])
    b = _dot(yattn_ref[...].astype(BF16), pattn_ref[...])
    merged = jax.nn.sigmoid(gates[:, :D_MODEL]) * a + jax.nn.sigmoid(gates[:, D_MODEL:]) * b
    x1 = x_ref[...] + gate1_ref[...] * _dot(merged.astype(BF16), wo_ref[...])
    x1_ref[...] = x1
    h2 = _rms(x1, g2_ref[...]) * (1.0 + scale2_ref[...]) + shift2_ref[...]
    h2_ref[...] = h2
    logit_ref[...] = jnp.dot(h2, wr_ref[...], preferred_element_type=F32,
                             precision=lax.Precision.HIGHEST) + br_ref[...]


def _post_mixer(yssm_tm, yattn, gate, x, gate1, shift2, scale2, weights, ts, per_token_mod):
    nb, s, _ = x.shape
    ns = s // ts
    p_ssm, p_attn, w_o, g2, w_r, b_r = weights
    if per_token_mod:
        mod_spec = pl.BlockSpec((ts, D_MODEL), lambda b, i: (b * ns + i, 0))
    else:
        mod_spec = pl.BlockSpec((None, 1, D_MODEL), lambda b, i: (b, 0, 0))
    const = lambda shape: pl.BlockSpec(shape, lambda b, i: (0,) * len(shape))
    tok = lambda w: pl.BlockSpec((None, ts, w), lambda b, i: (b, i, 0))
    return pl.pallas_call(
        _post_kernel, grid=(nb, ns),
        in_specs=[pl.BlockSpec((ts, SSM_WIDTH), lambda b, i: (i, b)), tok(ATTN_W), tok(2 * D_MODEL), tok(D_MODEL),
                  mod_spec, mod_spec, mod_spec,
                  const((SSM_WIDTH, D_MODEL)), const((ATTN_W, D_MODEL)), const((D_MODEL, D_MODEL)),
                  const((1, D_MODEL)), const((D_MODEL, N_EXPERTS)), const((1, N_EXPERTS))],
        out_specs=[tok(D_MODEL), tok(D_MODEL), tok(N_EXPERTS)],
        out_shape=[jax.ShapeDtypeStruct((nb, s, D_MODEL), F32), jax.ShapeDtypeStruct((nb, s, D_MODEL), F32),
                   jax.ShapeDtypeStruct((nb, s, N_EXPERTS), F32)],
        compiler_params=_cparams(("arbitrary", "arbitrary")),
        name="post_mixer",
    )(yssm_tm, yattn, gate, x, gate1, shift2, scale2, p_ssm, p_attn, w_o, g2, w_r, b_r)


def _row_copy(src_hbm, tok, dst_ref, row, sem):
    return pltpu.make_async_copy(src_hbm.at[pl.ds(tok, 1)], dst_ref.at[pl.ds(row, 1)], sem)


def _start_row_gather(src_hbm, idx_ref, dst_ref, sem, n):
    def body(r, carry):
        _row_copy(src_hbm, idx_ref[0, r], dst_ref, r, sem).start()
        return carry
    lax.fori_loop(0, n, body, 0, unroll=8)


def _wait_row_gather(src_hbm, dst_ref, sem, n):
    def body(r, carry):
        _row_copy(src_hbm, 0, dst_ref, r, sem).wait()
        return carry
    lax.fori_loop(0, n, body, 0, unroll=8)


def _moe_kernel(blk_exp_ref, n_used_ref, tok_ref, tok_next_ref, gate_ref, h_hbm, wup_ref, bup_ref,
                wdn_ref, bdn_ref, o_ref, xbuf_ref, sem_ref, *, blk):
    i = pl.program_id(0)
    n_used = n_used_ref[0]
    slot = i % 2

    @pl.when(i == 0)
    def _():
        _start_row_gather(h_hbm, tok_ref, xbuf_ref.at[0], sem_ref.at[0], blk)

    @pl.when(i + 1 < n_used)
    def _():
        _start_row_gather(h_hbm, tok_next_ref, xbuf_ref.at[1 - slot], sem_ref.at[1 - slot], blk)

    @pl.when(i < n_used)
    def _():
        _wait_row_gather(h_hbm, xbuf_ref.at[slot], sem_ref.at[slot], blk)
        xb = xbuf_ref[slot].astype(BF16)
        gu = _dot(xb, wup_ref[...]) + bup_ref[...]
        g_lin = jnp.minimum(gu[:, :D_FF], SWIGLU_LIMIT)
        u_lin = jnp.clip(gu[:, D_FF:], -SWIGLU_LIMIT, SWIGLU_LIMIT)
        act = g_lin * jax.nn.sigmoid(SWIGLU_ALPHA * g_lin) * (u_lin + 1.0)
        out = _dot(act.astype(BF16), wdn_ref[...]) + bdn_ref[...]
        o_ref[...] = out * gate_ref[...]

    @pl.when(jnp.logical_and(i >= n_used, i > 0))
    def _():
        o_ref[...] = jnp.zeros(o_ref.shape, F32)


def _routed_experts(h2, row_tok, row_gate, blk_exp, n_used, w_up, b_up, w_down, b_down, blk):
    n_rows = row_tok.shape[0]
    n_blocks = n_rows // blk
    tok3 = row_tok.reshape(n_blocks, 1, blk)
    kern = functools.partial(_moe_kernel, blk=blk)
    last = n_blocks - 1
    grid_spec = pltpu.PrefetchScalarGridSpec(
        num_scalar_prefetch=2,
        grid=(n_blocks,),
        in_specs=[pl.BlockSpec((None, 1, blk), lambda i, be, nu: (i, 0, 0), memory_space=pltpu.SMEM),
                  pl.BlockSpec((None, 1, blk), lambda i, be, nu: (jnp.minimum(i + 1, last), 0, 0),
                               memory_space=pltpu.SMEM),
                  pl.BlockSpec((blk, 1), lambda i, be, nu: (i, 0)),
                  pl.BlockSpec(memory_space=pl.ANY),
                  pl.BlockSpec((None, D_MODEL, 2 * D_FF), lambda i, be, nu: (be[i], 0, 0)),
                  pl.BlockSpec((None, 1, 2 * D_FF), lambda i, be, nu: (be[i], 0, 0)),
                  pl.BlockSpec((None, D_FF, D_MODEL), lambda i, be, nu: (be[i], 0, 0)),
                  pl.BlockSpec((None, 1, D_MODEL), lambda i, be, nu: (be[i], 0, 0))],
        out_specs=pl.BlockSpec((blk, D_MODEL), lambda i, be, nu: (i, 0)),
        scratch_shapes=[pltpu.VMEM((2, blk, D_MODEL), F32), pltpu.SemaphoreType.DMA((2,))])
    return pl.pallas_call(
        kern, grid_spec=grid_spec,
        out_shape=jax.ShapeDtypeStruct((n_rows, D_MODEL), F32),
        compiler_params=_cparams(("arbitrary",)),
        name="routed_experts",
    )(blk_exp, n_used, tok3, tok3, row_gate.reshape(n_rows, 1), h2, w_up,
      b_up.reshape(N_EXPERTS, 1, 2 * D_FF), w_down, b_down.reshape(N_EXPERTS, 1, D_MODEL))


def _route(logits, blk):
    n_tok = logits.shape[0]
    top_v, top_i = lax.top_k(logits, TOP_K)
    gates = jax.nn.softmax(top_v, axis=-1)
    n_assign = n_tok * TOP_K
    e_flat = top_i.reshape(n_assign)
    onehot = (e_flat[:, None] == jnp.arange(N_EXPERTS, dtype=e_flat.dtype)[None, :]).astype(jnp.int32)
    csum = jnp.cumsum(onehot, axis=0)
    counts = csum[-1]
    rank = jnp.sum(csum * onehot, axis=1) - 1
    padded = ((counts + blk - 1) // blk) * blk
    pend = jnp.cumsum(padded)
    pstart = pend - padded
    dest = (pstart[e_flat] + rank).astype(jnp.int32)
    n_rows = ((n_assign + blk - 1) // blk) * blk + N_EXPERTS * blk
    n_blocks = n_rows // blk
    t_flat = jnp.repeat(jnp.arange(n_tok, dtype=jnp.int32), TOP_K)
    row_tok = jnp.zeros((n_rows,), jnp.int32).at[dest].set(t_flat)
    row_gate = jnp.zeros((n_rows,), F32).at[dest].set(gates.reshape(n_assign))
    blk_exp = jnp.minimum(jnp.searchsorted(pend, jnp.arange(n_blocks) * blk, side='right'),
                          N_EXPERTS - 1).astype(jnp.int32)
    n_used = (pend[-1] // blk).astype(jnp.int32).reshape(1)
    return row_tok, row_gate, blk_exp, n_used, dest.reshape(n_tok, TOP_K)


def _final_kernel(pos_ref, pos_next_ref, x1_ref, gate2_ref, g_ref, rows_hbm, o_ref, gbuf_ref, sem_ref, *, ts):
    i = pl.program_id(0)
    n = pl.num_programs(0)
    slot = i % 2
    n_rows = ts * TOP_K

    @pl.when(i == 0)
    def _():
        _start_row_gather(rows_hbm, pos_ref, gbuf_ref.at[0], sem_ref.at[0], n_rows)

    @pl.when(i + 1 < n)
    def _():
        _start_row_gather(rows_hbm, pos_next_ref, gbuf_ref.at[1 - slot], sem_ref.at[1 - slot], n_rows)

    _wait_row_gather(rows_hbm, gbuf_ref.at[slot], sem_ref.at[slot], n_rows)
    ffn = gbuf_ref[slot, 0:ts, :]
    for k in range(1, TOP_K):
        ffn = ffn + gbuf_ref[slot, k * ts:(k + 1) * ts, :]
    x = x1_ref[...] + gate2_ref[...] * ffn
    o_ref[...] = _rms(x, g_ref[...])


def _combine_final(x1, gate2, final_g, rows, pos, ts, per_token_mod):
    nb, s, _ = x1.shape
    ns = s // ts
    n_tiles = nb * ns
    pos3 = pos.reshape(n_tiles, ts, TOP_K).transpose(0, 2, 1).reshape(n_tiles, 1, ts * TOP_K)
    if per_token_mod:
        mod_spec = pl.BlockSpec((ts, D_MODEL), lambda i: (i, 0))
    else:
        mod_spec = pl.BlockSpec((None, 1, D_MODEL), lambda i: (i // ns, 0, 0))
    kern = functools.partial(_final_kernel, ts=ts)
    last = n_tiles - 1
    return pl.pallas_call(
        kern, grid=(n_tiles,),
        in_specs=[pl.BlockSpec((None, 1, ts * TOP_K), lambda i: (i, 0, 0), memory_space=pltpu.SMEM),
                  pl.BlockSpec((None, 1, ts * TOP_K), lambda i: (jnp.minimum(i + 1, last), 0, 0),
                               memory_space=pltpu.SMEM),
                  pl.BlockSpec((None, ts, D_MODEL), lambda i: (i // ns, i % ns, 0)),
                  mod_spec,
                  pl.BlockSpec((1, D_MODEL), lambda i: (0, 0)),
                  pl.BlockSpec(memory_space=pl.ANY)],
        out_specs=pl.BlockSpec((None, ts, D_MODEL), lambda i: (i // ns, i % ns, 0)),
        out_shape=jax.ShapeDtypeStruct((nb, s, D_MODEL), F32),
        scratch_shapes=[pltpu.VMEM((2, ts * TOP_K, D_MODEL), F32), pltpu.SemaphoreType.DMA((2,))],
        compiler_params=_cparams(("arbitrary",)),
        name="combine_final",
    )(pos3, pos3, x1, gate2, final_g, rows)


def _group_forward(x, mod, pos, h0_re, h0_im, past, lam, wts, *, per_token_mod, ts, seq_major_tokens):
    nb, s, _ = x.shape
    shift1, scale1, gate1, shift2, scale2, gate2 = mod
    u_tm, q, kt, v, gate = _in_projection(x, shift1, scale1, wts['norm1_g'], wts['w_in_parts'],
                                          _rope_tables(pos), ts, per_token_mod)
    if past is None:
        y_attn = _prompt_attention(lam, q, kt, v, wts['subln_g'], tq=min(512, s))
        ssm_nb, ssm_len = nb, s
        u_scan = u_tm.reshape(s * nb, SSM_WIDTH)
    else:
        n_seq, seq_len = seq_major_tokens
        page_table, cache_kt, cache_v2 = past
        q_s = q.reshape(n_seq, seq_len, ATTN_W)
        kt_s = kt.reshape(ATTN_W, n_seq, seq_len).transpose(1, 0, 2)
        v_s = v.reshape(n_seq, seq_len, ATTN_W)
        y_attn = _sample_attention(page_table, lam, q_s, kt_s, v_s, wts['subln_g'], cache_kt, cache_v2,
                                   pages=8).reshape(nb, s, ATTN_W)
        ssm_nb, ssm_len = n_seq, seq_len
        u_scan = u_tm.reshape(n_seq, seq_len, SSM_WIDTH).transpose(1, 0, 2).reshape(s, SSM_WIDTH)
    y_tm, hT_re, hT_im = _ssm_scan(u_scan, h0_re, h0_im, wts['ssm_params'], wts['w_glu'], wts['b_glu'],
                                   nb=ssm_nb, steps=min(ssm_len, max(8, 512 // ssm_nb)))
    if past is None:
        y_ssm = y_tm.reshape(s, nb * SSM_WIDTH)
    else:
        y_ssm = y_tm.reshape(seq_len, n_seq, SSM_WIDTH).transpose(1, 0, 2).reshape(s, SSM_WIDTH)
    x1, h2, logits = _post_mixer(y_ssm, y_attn, gate, x, gate1, shift2, scale2, wts['post'], ts, per_token_mod)
    n_tok = nb * s
    blk = 256
    row_tok, row_gate, blk_exp, n_used, dest = _route(logits.reshape(n_tok, N_EXPERTS), blk)
    rows = _routed_experts(h2.reshape(n_tok, D_MODEL), row_tok, row_gate, blk_exp, n_used,
                           wts['w_up'], wts['b_up'], wts['w_down'], wts['b_down'], blk)
    y = _combine_final(x1, gate2, wts['final_g'], rows, dest, min(128, s), per_token_mod)
    return y, kt, v, hT_re, hT_im


def kernel(x_prompt, x_sample, c_prompt, c_sample, cache_k, cache_v, state_ssm_re, state_ssm_im, page_table, norm1_g, w_ada, b_ada, w_in, ssm_a_re, ssm_a_im, ssm_log_dt, ssm_b_re, ssm_b_im, ssm_c_re, ssm_c_im, ssm_d, w_glu, b_glu, lambda_q1, lambda_k1, lambda_q2, lambda_k2, subln_g, p_ssm, p_attn, w_o, norm2_g, w_router, b_router, w_up, b_up, w_down, b_down, final_g):
    bsz_p, seq_p, _ = x_prompt.shape
    bsz_s, seq_s, _ = x_sample.shape
    n_pool, page = cache_k.shape[1], cache_k.shape[2]
    past_len = page_table.shape[1] * page
    l = 0

    w = w_in[l]
    c0 = SSM_WIDTH
    w_in_parts = (w[:, :c0].astype(BF16), w[:, c0:c0 + ATTN_W].astype(BF16),
                  w[:, c0 + ATTN_W:c0 + 2 * ATTN_W].T.astype(BF16),
                  w[:, c0 + 2 * ATTN_W:c0 + 3 * ATTN_W].astype(BF16), w[:, c0 + 3 * ATTN_W:].astype(BF16))
    wts = {
        'norm1_g': norm1_g[l].reshape(1, D_MODEL),
        'w_in_parts': w_in_parts,
        'subln_g': subln_g[l].reshape(1, HEAD_W),
        'ssm_params': _ssm_params(ssm_a_re[l], ssm_a_im[l], ssm_log_dt[l], ssm_b_re[l], ssm_b_im[l],
                                  ssm_c_re[l], ssm_c_im[l], ssm_d[l]),
        'w_glu': w_glu[l].astype(BF16), 'b_glu': b_glu[l].reshape(1, SSM_WIDTH),
        'post': (p_ssm[l].astype(BF16), p_attn[l].astype(BF16), w_o[l].astype(BF16),
                 norm2_g[l].reshape(1, D_MODEL), w_router[l], b_router[l].reshape(1, N_EXPERTS)),
        'w_up': w_up[l].astype(BF16), 'b_up': b_up[l], 'w_down': w_down[l].astype(BF16), 'b_down': b_down[l],
        'final_g': final_g.reshape(1, D_MODEL),
    }
    lam = (jnp.exp(jnp.sum(lambda_q1[l].astype(F32) * lambda_k1[l].astype(F32)))
           - jnp.exp(jnp.sum(lambda_q2[l].astype(F32) * lambda_k2[l].astype(F32))) + LAM_INIT).reshape(1)

    mod = _modulation(jnp.concatenate([c_prompt, c_sample], axis=0), w_ada[l], b_ada[l])
    mod_p = [m.reshape(bsz_p, 1, D_MODEL) for m in jnp.split(mod[:bsz_p], 6, axis=-1)]
    mod_s = [jnp.repeat(m, seq_s, axis=0) for m in jnp.split(mod[bsz_p:], 6, axis=-1)]

    zeros_state = jnp.zeros((bsz_p, SSM_LANES), F32)
    yp, ktp, vp, rp, ip = _group_forward(
        x_prompt, mod_p, jnp.arange(seq_p), zeros_state, zeros_state, None, lam, wts,
        per_token_mod=False, ts=min(256, seq_p), seq_major_tokens=None)

    cache_kt = jnp.transpose(cache_k[l], (0, 2, 3, 1)).reshape(n_pool, ATTN_W, page)
    cache_v2 = cache_v[l].reshape(n_pool, page * N_HEADS, HEAD_W)
    n_s = bsz_s * seq_s
    pos_s = past_len + jnp.tile(jnp.arange(seq_s), bsz_s)
    ys, kts, vs, rs, is_ = _group_forward(
        x_sample.reshape(1, n_s, D_MODEL), mod_s, pos_s,
        state_ssm_re[l].reshape(bsz_s, SSM_LANES), state_ssm_im[l].reshape(bsz_s, SSM_LANES),
        (page_table, cache_kt, cache_v2), lam, wts,
        per_token_mod=True, ts=n_s, seq_major_tokens=(bsz_s, seq_s))

    k_prompt = jnp.transpose(ktp.reshape(bsz_p, 2 * N_HEADS, HEAD_DIM, seq_p), (0, 3, 1, 2))[None]
    v_prompt = vp.reshape(bsz_p, seq_p, N_HEADS, HEAD_W)[None]
    k_sample = jnp.transpose(kts.reshape(2 * N_HEADS, HEAD_DIM, bsz_s, seq_s), (2, 3, 0, 1))[None]
    v_sample = vs.reshape(bsz_s, seq_s, N_HEADS, HEAD_W)[None]
    state = lambda h, n: h.reshape(n, SSM_GROUPS, SSM_STATE)[None]
    return (yp, ys.reshape(bsz_s, seq_s, D_MODEL), k_prompt, v_prompt, state(rp, bsz_p), state(ip, bsz_p),
            k_sample, v_sample, state(rs, bsz_s), state(is_, bsz_s))
```

```python
import functools
import math

import jax
import jax.numpy as jnp
from jax import lax
from jax.experimental import pallas as pl
from jax.experimental.pallas import tpu as pltpu

D_MODEL = 1024
SSM_WIDTH = D_MODEL // 2
SSM_GROUP = 16
SSM_GROUPS = SSM_WIDTH // SSM_GROUP
SSM_STATE = 64
SSM_LANES = SSM_GROUPS * SSM_STATE
MIN_DECAY = 1e-4
N_HEADS = 4
HEAD_DIM = 64
HEAD_W = 2 * HEAD_DIM
ATTN_W = N_HEADS * HEAD_W
ROT_DIM = HEAD_DIM // 4
ROT_HALF = ROT_DIM // 2
ROPE_THETA = 500000.0
NEG_INF = -1e30
N_EXPERTS = 32
TOP_K = 4
D_FF = D_MODEL
SWIGLU_LIMIT = 7.0
SWIGLU_ALPHA = 1.702
EPS = 1e-6
LAM_INIT = 0.8 - 0.6 * math.exp(-0.3 * 0)
LOG2_E = 1.4426950408889634

VMEM_LIMIT_BYTES = 56 * 1024 * 1024
LANE = 128

BF16 = jnp.bfloat16
F32 = jnp.float32


def _cparams(sem):
    return pltpu.CompilerParams(dimension_semantics=sem, vmem_limit_bytes=VMEM_LIMIT_BYTES)


def _dot(a, b):
    return jnp.dot(a, b, preferred_element_type=F32)


def _rms(x, g):
    return x * lax.rsqrt(jnp.mean(x * x, axis=-1, keepdims=True) + EPS) * g


def _mod_kernel(c_ref, w_ref, b_ref, o_ref):
    c = c_ref[...]
    s = (c * jax.nn.sigmoid(c)).astype(BF16)
    o_ref[...] = _dot(s, w_ref[...].astype(BF16)) + b_ref[...]


def _modulation(c, w_ada, b_ada):
    n = c.shape[0]
    return pl.pallas_call(
        _mod_kernel,
        grid=(6,),
        in_specs=[pl.BlockSpec((n, D_MODEL), lambda j: (0, 0)),
                  pl.BlockSpec((D_MODEL, D_MODEL), lambda j: (0, j)),
                  pl.BlockSpec((1, D_MODEL), lambda j: (0, j))],
        out_specs=pl.BlockSpec((n, D_MODEL), lambda j: (0, j)),
        out_shape=jax.ShapeDtypeStruct((n, 6 * D_MODEL), F32),
        compiler_params=_cparams(("arbitrary",)),
        name="adaln_mod",
    )(c, w_ada, b_ada.reshape(1, 6 * D_MODEL))


def _inproj_kernel(x_ref, shift_ref, scale_ref, g_ref, wu_ref, wq_ref, wkt_ref, wv_ref, wg_ref,
                   cosq_ref, sina_ref, sinb_ref, cost_ref, sint_ref,
                   u_ref, q_ref, kt_ref, v_ref, gate_ref):
    x = x_ref[...]
    h = _rms(x, g_ref[...]) * (1.0 + scale_ref[...]) + shift_ref[...]
    hb = h.astype(BF16)
    u_ref[...] = _dot(hb, wu_ref[...])
    v_ref[...] = _dot(hb, wv_ref[...])
    gate_ref[...] = _dot(hb, wg_ref[...])
    q = _dot(hb, wq_ref[...])
    blocks = [q[:, j:j + LANE] for j in range(0, ATTN_W, LANE)]
    q_up = jnp.concatenate([pltpu.roll(b, LANE - ROT_HALF, 1) for b in blocks], axis=1)
    q_dn = jnp.concatenate([pltpu.roll(b, ROT_HALF, 1) for b in blocks], axis=1)
    q = q * cosq_ref[...] + q_up * sina_ref[...] + q_dn * sinb_ref[...]
    q_ref[...] = (q * (HEAD_DIM ** -0.5 * LOG2_E)).astype(BF16)
    kt = lax.dot_general(wkt_ref[...], hb, (((1,), (1,)), ((), ())), preferred_element_type=F32)
    cos_t = cost_ref[...]
    sin_t = sint_ref[...]
    for hh in range(2 * N_HEADS):
        r0 = hh * HEAD_DIM
        x1 = kt[r0:r0 + ROT_HALF, :]
        x2 = kt[r0 + ROT_HALF:r0 + ROT_DIM, :]
        kt_ref[r0:r0 + ROT_HALF, :] = x1 * cos_t - x2 * sin_t
        kt_ref[r0 + ROT_HALF:r0 + ROT_DIM, :] = x2 * cos_t + x1 * sin_t
        kt_ref[r0 + ROT_DIM:r0 + HEAD_DIM, :] = kt[r0 + ROT_DIM:r0 + HEAD_DIM, :]


def _rope_tables(pos):
    inv_freq = ROPE_THETA ** (-jnp.arange(ROT_HALF, dtype=F32) * (2.0 / ROT_DIM))
    ang = pos.astype(F32)[:, None] * inv_freq[None, :]
    cos, sin = jnp.cos(ang), jnp.sin(ang)
    s = pos.shape[0]
    pad = HEAD_DIM - ROT_DIM
    cos_h = jnp.concatenate([cos, cos, jnp.ones((s, pad), F32)], axis=1)
    sina_h = jnp.concatenate([-sin, jnp.zeros((s, ROT_HALF + pad), F32)], axis=1)
    sinb_h = jnp.concatenate([jnp.zeros((s, ROT_HALF), F32), sin, jnp.zeros((s, pad), F32)], axis=1)
    reps = ATTN_W // HEAD_DIM
    return (jnp.tile(cos_h, (1, reps)), jnp.tile(sina_h, (1, reps)), jnp.tile(sinb_h, (1, reps)),
            cos.T, sin.T)


def _in_projection(x, shift, scale, g, weights, tables, ts, per_token_mod):
    nb, s, _ = x.shape
    ns = s // ts
    wu, wq, wkt, wv, wg = weights
    cosq, sina, sinb, cost, sint = tables
    if per_token_mod:
        mod_spec = pl.BlockSpec((ts, D_MODEL), lambda b, i: (b * ns + i, 0))
    else:
        mod_spec = pl.BlockSpec((None, 1, D_MODEL), lambda b, i: (b, 0, 0))
    const = lambda shape: pl.BlockSpec(shape, lambda b, i: (0,) * len(shape))
    tab = pl.BlockSpec((ts, ATTN_W), lambda b, i: (i, 0))
    tab_t = pl.BlockSpec((ROT_HALF, ts), lambda b, i: (0, i))
    return pl.pallas_call(
        _inproj_kernel,
        grid=(nb, ns),
        in_specs=[pl.BlockSpec((None, ts, D_MODEL), lambda b, i: (b, i, 0)), mod_spec, mod_spec,
                  const((1, D_MODEL)),
                  const((D_MODEL, SSM_WIDTH)), const((D_MODEL, ATTN_W)), const((ATTN_W, D_MODEL)),
                  const((D_MODEL, ATTN_W)), const((D_MODEL, 2 * D_MODEL)),
                  tab, tab, tab, tab_t, tab_t],
        out_specs=[pl.BlockSpec((ts, SSM_WIDTH), lambda b, i: (i, b)),
                   pl.BlockSpec((None, ts, ATTN_W), lambda b, i: (b, i, 0)),
                   pl.BlockSpec((None, ATTN_W, ts), lambda b, i: (b, 0, i)),
                   pl.BlockSpec((None, ts, ATTN_W), lambda b, i: (b, i, 0)),
                   pl.BlockSpec((None, ts, 2 * D_MODEL), lambda b, i: (b, i, 0))],
        out_shape=[jax.ShapeDtypeStruct((s, nb * SSM_WIDTH), F32),
                   jax.ShapeDtypeStruct((nb, s, ATTN_W), BF16),
                   jax.ShapeDtypeStruct((nb, ATTN_W, s), F32),
                   jax.ShapeDtypeStruct((nb, s, ATTN_W), F32),
                   jax.ShapeDtypeStruct((nb, s, 2 * D_MODEL), F32)],
        compiler_params=_cparams(("arbitrary", "arbitrary")),
        name="in_proj",
    )(x, shift, scale, g, wu, wq, wkt, wv, wg, cosq, sina, sinb, cost, sint)


def _softmax_step(s, v_bf, m_prev, l_prev, acc_prev):
    m_next = jnp.maximum(m_prev, jnp.max(s, axis=1, keepdims=True))
    p = jnp.exp2(s - jnp.tile(m_next, (1, s.shape[1] // LANE)))
    alpha = jnp.exp2(m_prev - m_next)
    l_next = alpha * l_prev + jnp.sum(p, axis=1, keepdims=True)
    acc_next = alpha * acc_prev + _dot(p.astype(BF16), v_bf)
    return m_next, l_next, acc_next


def _diff_head_out(lam, acc0, l0, acc1, l1, g):
    o = acc0 / l0 - lam * (acc1 / l1)
    return _rms(o, g) * (1.0 - LAM_INIT)


def _init_softmax_state(m_ref, l_ref, acc_ref):
    m_ref[...] = jnp.full(m_ref.shape, NEG_INF, F32)
    l_ref[...] = jnp.zeros(l_ref.shape, F32)
    acc_ref[...] = jnp.zeros(acc_ref.shape, F32)


def _prompt_attn_kernel(qi_ref, ki_ref, lam_ref, q_ref, kt_ref, v_ref, g_ref, o_ref, m_ref, l_ref, acc_ref, *, t):
    pair = pl.program_id(1)
    qi = qi_ref[pair]
    ki = ki_ref[pair]

    @pl.when(ki == 0)
    def _():
        _init_softmax_state(m_ref, l_ref, acc_ref)

    def update(masked):
        if masked:
            visible = (lax.broadcasted_iota(jnp.int32, (t, t), 1) <= lax.broadcasted_iota(jnp.int32, (t, t), 0))
        for h in range(N_HEADS):
            v_bf = v_ref[:, h * HEAD_W:(h + 1) * HEAD_W].astype(BF16)
            for c in range(2):
                hc = 2 * h + c
                r0 = hc * HEAD_DIM
                s = _dot(q_ref[:, r0:r0 + HEAD_DIM], kt_ref[r0:r0 + HEAD_DIM, :].astype(BF16))
                if masked:
                    s = jnp.where(visible, s, NEG_INF)
                m_ref[hc], l_ref[hc], acc_ref[hc] = _softmax_step(s, v_bf, m_ref[hc], l_ref[hc], acc_ref[hc])

    @pl.when(ki < qi)
    def _():
        update(False)

    @pl.when(ki == qi)
    def _():
        update(True)
        lam = lam_ref[0]
        g = g_ref[...]
        for h in range(N_HEADS):
            o_ref[:, h * HEAD_W:(h + 1) * HEAD_W] = _diff_head_out(
                lam, acc_ref[2 * h], l_ref[2 * h], acc_ref[2 * h + 1], l_ref[2 * h + 1], g)


def _prompt_attention(lam, q, kt, v, subln_g, t):
    b, s, _ = q.shape
    n = s // t
    pairs = [(qi, ki) for qi in range(n) for ki in range(qi + 1)]
    qi_tab = jnp.asarray([p[0] for p in pairs], jnp.int32)
    ki_tab = jnp.asarray([p[1] for p in pairs], jnp.int32)
    kern = functools.partial(_prompt_attn_kernel, t=t)
    grid_spec = pltpu.PrefetchScalarGridSpec(
        num_scalar_prefetch=3,
        grid=(b, len(pairs)),
        in_specs=[pl.BlockSpec((None, t, ATTN_W), lambda bi, p, qt, kt_, lam_: (bi, qt[p], 0)),
                  pl.BlockSpec((None, ATTN_W, t), lambda bi, p, qt, kt_, lam_: (bi, 0, kt_[p])),
                  pl.BlockSpec((None, t, ATTN_W), lambda bi, p, qt, kt_, lam_: (bi, kt_[p], 0)),
                  pl.BlockSpec((1, HEAD_W), lambda bi, p, qt, kt_, lam_: (0, 0))],
        out_specs=pl.BlockSpec((None, t, ATTN_W), lambda bi, p, qt, kt_, lam_: (bi, qt[p], 0)),
        scratch_shapes=[pltpu.VMEM((2 * N_HEADS, t, LANE), F32), pltpu.VMEM((2 * N_HEADS, t, LANE), F32),
                        pltpu.VMEM((2 * N_HEADS, t, HEAD_W), F32)])
    return pl.pallas_call(
        kern, grid_spec=grid_spec,
        out_shape=jax.ShapeDtypeStruct((b, s, ATTN_W), F32),
        compiler_params=_cparams(("arbitrary", "arbitrary")),
        name="prompt_attn",
    )(qi_tab, ki_tab, lam, q, kt, v, subln_g)


def _sample_attn_kernel(pt_ref, lam_ref, q_ref, ktn_ref, vn_ref, g_ref, *rest, pages, page):
    k_refs = rest[:pages]
    v_refs = rest[pages:2 * pages]
    o_ref, kcat_ref, vcat_ref, m_ref, l_ref, acc_ref = rest[2 * pages:]
    j = pl.program_id(1)
    nq = q_ref.shape[0]
    hrows = 2 * nq

    @pl.when(j == 0)
    def _():
        _init_softmax_state(m_ref, l_ref, acc_ref)

    for i in range(pages):
        kcat_ref[:, i * page:(i + 1) * page] = k_refs[i][...].astype(BF16)
        for h in range(N_HEADS):
            vcat_ref[h, i * page:(i + 1) * page, :] = v_refs[i][pl.ds(h, page, stride=N_HEADS), :].astype(BF16)

    q = q_ref[...]
    q_parts = [q[:, hc * HEAD_DIM:(hc + 1) * HEAD_DIM] for hc in range(2 * N_HEADS)]
    s = jnp.concatenate([_dot(q_parts[hc], kcat_ref[hc * HEAD_DIM:(hc + 1) * HEAD_DIM, :])
                         for hc in range(2 * N_HEADS)], axis=0)
    m_prev = m_ref[...]
    m_next = jnp.maximum(m_prev, jnp.max(s, axis=1, keepdims=True))
    p = jnp.exp2(s - jnp.tile(m_next, (1, s.shape[1] // LANE)))
    alpha = jnp.exp2(m_prev - m_next)
    l_ref[...] = alpha * l_ref[...] + jnp.sum(p, axis=1, keepdims=True)
    pb = p.astype(BF16)
    pv = jnp.concatenate([_dot(pb[h * hrows:(h + 1) * hrows], vcat_ref[h]) for h in range(N_HEADS)], axis=0)
    acc_ref[...] = alpha * acc_ref[...] + pv
    m_ref[...] = m_next

    @pl.when(j == pl.num_programs(1) - 1)
    def _():
        rows = 2 * N_HEADS * nq
        s_new = jnp.concatenate([_dot(q_parts[hc], ktn_ref[hc * HEAD_DIM:(hc + 1) * HEAD_DIM, :].astype(BF16))
                                 for hc in range(2 * N_HEADS)], axis=0)
        qpos = lax.broadcasted_iota(jnp.int32, (rows, nq), 0) % nq
        kpos = lax.broadcasted_iota(jnp.int32, (rows, nq), 1)
        s_new = jnp.where(kpos <= qpos, s_new, NEG_INF)
        m_old = m_ref[...]
        m_fin = jnp.maximum(m_old, jnp.max(s_new, axis=1, keepdims=True))
        p_new = jnp.exp2(s_new - m_fin[:, :nq])
        a_fin = jnp.exp2(m_old - m_fin)
        l_fin = a_fin * l_ref[...] + jnp.sum(p_new, axis=1, keepdims=True)
        pnb = p_new.astype(BF16)
        pv_new = jnp.concatenate(
            [_dot(pnb[h * hrows:(h + 1) * hrows], vn_ref[:, h * HEAD_W:(h + 1) * HEAD_W].astype(BF16))
             for h in range(N_HEADS)], axis=0)
        acc_fin = a_fin * acc_ref[...] + pv_new
        lam = lam_ref[0]
        g = g_ref[...]
        for h in range(N_HEADS):
            r0 = h * hrows
            o_ref[:, h * HEAD_W:(h + 1) * HEAD_W] = _diff_head_out(
                lam, acc_fin[r0:r0 + nq], l_fin[r0:r0 + nq], acc_fin[r0 + nq:r0 + hrows], l_fin[r0 + nq:r0 + hrows], g)


def _sample_attention(page_table, lam, q, kt_new, v_new, subln_g, cache_kt, cache_v2, pages):
    b, nq, _ = q.shape
    n_pages = page_table.shape[1]
    page = cache_kt.shape[2]
    nj = n_pages // pages
    rows = 2 * N_HEADS * nq
    kern = functools.partial(_sample_attn_kernel, pages=pages, page=page)

    def k_spec(i):
        return pl.BlockSpec((None, ATTN_W, page), lambda bi, j, pt, lam_: (pt[bi, j * pages + i], 0, 0))

    def v_spec(i):
        return pl.BlockSpec((None, page * N_HEADS, HEAD_W), lambda bi, j, pt, lam_: (pt[bi, j * pages + i], 0, 0))

    grid_spec = pltpu.PrefetchScalarGridSpec(
        num_scalar_prefetch=2,
        grid=(b, nj),
        in_specs=[pl.BlockSpec((None, nq, ATTN_W), lambda bi, j, pt, lam_: (bi, 0, 0)),
                  pl.BlockSpec((None, ATTN_W, nq), lambda bi, j, pt, lam_: (bi, 0, 0)),
                  pl.BlockSpec((None, nq, ATTN_W), lambda bi, j, pt, lam_: (bi, 0, 0)),
                  pl.BlockSpec((1, HEAD_W), lambda bi, j, pt, lam_: (0, 0))]
                 + [k_spec(i) for i in range(pages)] + [v_spec(i) for i in range(pages)],
        out_specs=pl.BlockSpec((None, nq, ATTN_W), lambda bi, j, pt, lam_: (bi, 0, 0)),
        scratch_shapes=[pltpu.VMEM((ATTN_W, pages * page), BF16),
                        pltpu.VMEM((N_HEADS, pages * page, HEAD_W), BF16),
                        pltpu.VMEM((rows, LANE), F32), pltpu.VMEM((rows, LANE), F32),
                        pltpu.VMEM((rows, HEAD_W), F32)])
    return pl.pallas_call(
        kern, grid_spec=grid_spec,
        out_shape=jax.ShapeDtypeStruct((b, nq, ATTN_W), F32),
        compiler_params=_cparams(("arbitrary", "arbitrary")),
        name="sample_attn",
    )(page_table, lam, q, kt_new, v_new, subln_g, *([cache_kt] * pages), *([cache_v2] * pages))


def _ssm_kernel(u_ref, h0r_ref, h0i_ref, ar_ref, ai_ref, bbr_ref, bbi_ref, ccr_ref, cci_ref, d_ref,
                wglu_ref, bglu_ref, y_ref, hr_out, hi_out, sr_ref, si_ref, br_ref, bi_ref,
                *, steps, nb, lane_chunk):
    i = pl.program_id(0)

    @pl.when(i == 0)
    def _():
        sr_ref[...] = h0r_ref[...]
        si_ref[...] = h0i_ref[...]

    u = u_ref[...]
    ub = u.astype(BF16)
    br_ref[...] = _dot(ub, bbr_ref[...])
    bi_ref[...] = _dot(ub, bbi_ref[...])

    for c0 in range(0, SSM_LANES, lane_chunk):
        lanes = pl.ds(c0, lane_chunk)
        a_r = jnp.broadcast_to(ar_ref[:, lanes], (nb, lane_chunk))
        a_i = jnp.broadcast_to(ai_ref[:, lanes], (nb, lane_chunk))

        def step(t, carry):
            h_r, h_i = carry
            rows = pl.ds(pl.multiple_of(t * nb, nb), nb)
            n_r = a_r * h_r - a_i * h_i + br_ref[rows, lanes]
            n_i = a_r * h_i + a_i * h_r + bi_ref[rows, lanes]
            br_ref[rows, lanes] = n_r
            bi_ref[rows, lanes] = n_i
            return n_r, n_i

        h_r, h_i = lax.fori_loop(0, steps, step, (sr_ref[:, lanes], si_ref[:, lanes]), unroll=4)
        sr_ref[:, lanes] = h_r
        si_ref[:, lanes] = h_i

    y = (_dot(br_ref[...].astype(BF16), ccr_ref[...]) - _dot(bi_ref[...].astype(BF16), cci_ref[...])
         + d_ref[...] * u)
    y = jax.nn.gelu(y, approximate=True)
    y_ref[...] = y * jax.nn.sigmoid(_dot(y.astype(BF16), wglu_ref[...]) + bglu_ref[...])

    @pl.when(i == pl.num_programs(0) - 1)
    def _():
        hr_out[...] = sr_ref[...]
        hi_out[...] = si_ref[...]


def _ssm_params(a_re, a_im, log_dt, b_re, b_im, c_re, c_im, d_skip):
    lam_re = jnp.minimum(a_re.astype(F32), -MIN_DECAY)
    lam_im = a_im.astype(F32)
    dt = jnp.exp(log_dt.astype(F32))[:, None]
    mag = jnp.exp(lam_re * dt)
    ab_re = mag * jnp.cos(lam_im * dt)
    ab_im = mag * jnp.sin(lam_im * dt)
    den = lam_re * lam_re + lam_im * lam_im
    n_re = ab_re - 1.0
    n_im = ab_im
    zoh_re = (n_re * lam_re + n_im * lam_im) / den
    zoh_im = (n_im * lam_re - n_re * lam_im) / den
    br, bi = b_re.astype(F32), b_im.astype(F32)
    bb_re = zoh_re[..., None] * br - zoh_im[..., None] * bi
    bb_im = zoh_re[..., None] * bi + zoh_im[..., None] * br
    eye = jnp.eye(SSM_GROUPS, dtype=F32)

    def in_blockdiag(bb):
        m = jnp.einsum('gpc,gh->gchp', bb, eye)
        return m.reshape(SSM_WIDTH, SSM_LANES).astype(BF16)

    def out_blockdiag(cc):
        m = jnp.einsum('gcp,gh->gphc', cc.astype(F32), eye)
        return m.reshape(SSM_LANES, SSM_WIDTH).astype(BF16)

    return (ab_re.reshape(1, SSM_LANES), ab_im.reshape(1, SSM_LANES), in_blockdiag(bb_re), in_blockdiag(bb_im),
            out_blockdiag(c_re), out_blockdiag(c_im), d_skip.astype(F32).reshape(1, SSM_WIDTH))


def _ssm_scan(u_tm, h0_re, h0_im, params, w_glu, b_glu, nb, steps):
    rows_total = u_tm.shape[0]
    rows = steps * nb
    n_chunks = rows_total // rows
    ab_re, ab_im, bbr, bbi, ccr, cci, d_row = params
    lane_chunk = min(SSM_LANES, 8 * 1024 // nb)
    kern = functools.partial(_ssm_kernel, steps=steps, nb=nb, lane_chunk=lane_chunk)
    const = lambda shape: pl.BlockSpec(shape, lambda i: (0,) * len(shape))
    return pl.pallas_call(
        kern, grid=(n_chunks,),
        in_specs=[pl.BlockSpec((rows, SSM_WIDTH), lambda i: (i, 0)),
                  const((nb, SSM_LANES)), const((nb, SSM_LANES)),
                  const((1, SSM_LANES)), const((1, SSM_LANES)),
                  const((SSM_WIDTH, SSM_LANES)), const((SSM_WIDTH, SSM_LANES)),
                  const((SSM_LANES, SSM_WIDTH)), const((SSM_LANES, SSM_WIDTH)),
                  const((1, SSM_WIDTH)), const((SSM_WIDTH, SSM_WIDTH)), const((1, SSM_WIDTH))],
        out_specs=[pl.BlockSpec((rows, SSM_WIDTH), lambda i: (i, 0)),
                   const((nb, SSM_LANES)), const((nb, SSM_LANES))],
        out_shape=[jax.ShapeDtypeStruct((rows_total, SSM_WIDTH), F32),
                   jax.ShapeDtypeStruct((nb, SSM_LANES), F32), jax.ShapeDtypeStruct((nb, SSM_LANES), F32)],
        scratch_shapes=[pltpu.VMEM((nb, SSM_LANES), F32), pltpu.VMEM((nb, SSM_LANES), F32),
                        pltpu.VMEM((rows, SSM_LANES), F32), pltpu.VMEM((rows, SSM_LANES), F32)],
        compiler_params=_cparams(("arbitrary",)),
        name="ssm_scan",
    )(u_tm, h0_re, h0_im, ab_re, ab_im, bbr, bbi, ccr, cci, d_row, w_glu, b_glu)


def _post_kernel(yssm_ref, yattn_ref, gate_ref, x_ref, gate1_ref, shift2_ref, scale2_ref,
                 pssm_ref, pattn_ref, wo_ref, g2_ref, wr_ref, br_ref,
                 x1_ref, h2_ref, ridx_ref, rgate_ref, cnt_ref, carry_ref, *, ts):
    @pl.when(jnp.logical_and(pl.program_id(0) == 0, pl.program_id(1) == 0))
    def _():
        carry_ref[...] = jnp.zeros(carry_ref.shape, F32)

    gates = gate_ref[...]
    a = _dot(yssm_ref[...].astype(BF16), pssm_ref[...])
    b = _dot(yattn_ref[...].astype(BF16), pattn_ref[...])
    merged = jax.nn.sigmoid(gates[:, :D_MODEL]) * a + jax.nn.sigmoid(gates[:, D_MODEL:]) * b
    x1 = x_ref[...] + gate1_ref[...] * _dot(merged.astype(BF16), wo_ref[...])
    x1_ref[...] = x1
    h2 = _rms(x1, g2_ref[...]) * (1.0 + scale2_ref[...]) + shift2_ref[...]
    h2_ref[...] = h2
    logits = jnp.dot(h2, wr_ref[...], preferred_element_type=F32,
                     precision=lax.Precision.HIGHEST) + br_ref[...]

    lane = lax.broadcasted_iota(jnp.int32, (ts, N_EXPERTS), 1)
    work = logits
    idxs, vals = [], []
    for _ in range(TOP_K):
        mk = jnp.max(work, axis=1, keepdims=True)
        ik = jnp.min(jnp.where(work == mk, lane, N_EXPERTS), axis=1, keepdims=True)
        idxs.append(ik)
        vals.append(mk)
        work = jnp.where(lane == ik, -jnp.inf, work)
    exps = [jnp.exp(v - vals[0]) for v in vals]
    den = exps[0] + exps[1] + exps[2] + exps[3]
    rgate_ref[...] = jnp.concatenate([e / den for e in exps], axis=1)

    chosen = jnp.zeros((ts, N_EXPERTS), F32)
    for ik in idxs:
        chosen = chosen + jnp.where(lane == ik, 1.0, 0.0)
    tri = jnp.where(lax.broadcasted_iota(jnp.int32, (ts, ts), 1) < lax.broadcasted_iota(jnp.int32, (ts, ts), 0),
                    1.0, 0.0).astype(BF16)
    before = _dot(tri, chosen.astype(BF16)) + carry_ref[...]
    ranks = [jnp.sum(jnp.where(lane == ik, before, 0.0), axis=1, keepdims=True).astype(jnp.int32) for ik in idxs]
    ridx_ref[...] = jnp.concatenate(idxs + ranks, axis=1)
    carry_ref[...] = carry_ref[...] + jnp.sum(chosen, axis=0, keepdims=True)
    cnt_ref[...] = carry_ref[...]


def _post_mixer(yssm_tm, yattn, gate, x, gate1, shift2, scale2, weights, ts, per_token_mod):
    nb, s, _ = x.shape
    ns = s // ts
    p_ssm, p_attn, w_o, g2, w_r, b_r = weights
    if per_token_mod:
        mod_spec = pl.BlockSpec((ts, D_MODEL), lambda b, i: (b * ns + i, 0))
    else:
        mod_spec = pl.BlockSpec((None, 1, D_MODEL), lambda b, i: (b, 0, 0))
    const = lambda shape: pl.BlockSpec(shape, lambda b, i: (0,) * len(shape))
    tok = lambda w: pl.BlockSpec((None, ts, w), lambda b, i: (b, i, 0))
    return pl.pallas_call(
        functools.partial(_post_kernel, ts=ts), grid=(nb, ns),
        in_specs=[pl.BlockSpec((ts, SSM_WIDTH), lambda b, i: (i, b)), tok(ATTN_W), tok(2 * D_MODEL), tok(D_MODEL),
                  mod_spec, mod_spec, mod_spec,
                  const((SSM_WIDTH, D_MODEL)), const((ATTN_W, D_MODEL)), const((D_MODEL, D_MODEL)),
                  const((1, D_MODEL)), const((D_MODEL, N_EXPERTS)), const((1, N_EXPERTS))],
        out_specs=[tok(D_MODEL), tok(D_MODEL), tok(2 * TOP_K), tok(TOP_K), const((1, N_EXPERTS))],
        out_shape=[jax.ShapeDtypeStruct((nb, s, D_MODEL), F32), jax.ShapeDtypeStruct((nb, s, D_MODEL), F32),
                   jax.ShapeDtypeStruct((nb, s, 2 * TOP_K), jnp.int32),
                   jax.ShapeDtypeStruct((nb, s, TOP_K), F32),
                   jax.ShapeDtypeStruct((1, N_EXPERTS), F32)],
        scratch_shapes=[pltpu.VMEM((1, N_EXPERTS), F32)],
        compiler_params=_cparams(("arbitrary", "arbitrary")),
        name="post_mixer",
    )(yssm_tm, yattn, gate, x, gate1, shift2, scale2, p_ssm, p_attn, w_o, g2, w_r, b_r)


def _row_copy(src_hbm, src_row, dst_ref, dst_row, sem):
    return pltpu.make_async_copy(src_hbm.at[pl.ds(src_row, 1)], dst_ref.at[pl.ds(dst_row, 1)], sem)


def _start_rows(src_hbm, idx_ref, dst_ref, sem, n):
    for r in range(n):
        _row_copy(src_hbm, idx_ref[0, r], dst_ref, r, sem).start(priority=r % 2)


def _wait_rows(src_hbm, dst_ref, sem, n):
    for r in range(n):
        _row_copy(src_hbm, 0, dst_ref, r, sem).wait()


def _moe_kernel(iblk_ref, iexp_ref, ilo_ref, ihi_ref, tok_ref, tok_next_ref, h_hbm, wup_ref, bup_ref,
                wdn_ref, bdn_ref, o_ref, xbuf0_ref, xbuf1_ref, sem_ref, *, blk):
    w = pl.program_id(0)
    last = pl.num_programs(0) - 1

    @pl.when(w == 0)
    def _():
        _start_rows(h_hbm, tok_ref, xbuf0_ref, sem_ref.at[0], blk)

    def step(cur_ref, cur_sem, nxt_ref, nxt_sem):
        _wait_rows(h_hbm, cur_ref, cur_sem, blk)
        _start_rows(h_hbm, tok_next_ref, nxt_ref, nxt_sem, blk)
        xb = cur_ref[...].astype(BF16)
        gu = _dot(xb, wup_ref[...]) + bup_ref[...]
        g_lin = jnp.minimum(gu[:, :D_FF], SWIGLU_LIMIT)
        u_lin = jnp.clip(gu[:, D_FF:], -SWIGLU_LIMIT, SWIGLU_LIMIT)
        act = g_lin * jax.nn.sigmoid(SWIGLU_ALPHA * g_lin) * (u_lin + 1.0)
        res = _dot(act.astype(BF16), wdn_ref[...]) + bdn_ref[...]

        rows = iblk_ref[w] * blk + lax.broadcasted_iota(jnp.int32, (blk, 1), 0)
        mine = jnp.logical_and(rows >= ilo_ref[w], rows < ihi_ref[w])
        first = jnp.logical_or(w == 0, iblk_ref[w] != iblk_ref[jnp.maximum(w - 1, 0)])

        @pl.when(first)
        def _():
            o_ref[...] = jnp.where(mine, res, 0.0)

        @pl.when(jnp.logical_not(first))
        def _():
            o_ref[...] = jnp.where(mine, res, o_ref[...])

        @pl.when(w == last)
        def _():
            _wait_rows(h_hbm, nxt_ref, nxt_sem, blk)

    @pl.when(w % 2 == 0)
    def _():
        step(xbuf0_ref, sem_ref.at[0], xbuf1_ref, sem_ref.at[1])

    @pl.when(w % 2 == 1)
    def _():
        step(xbuf1_ref, sem_ref.at[1], xbuf0_ref, sem_ref.at[0])


def _routed_experts(h2, row_tok, plan, w_up, b_up, w_down, b_down, blk):
    item_blk, item_exp, item_lo, item_hi = plan
    n_rows = row_tok.shape[0]
    n_blocks = n_rows // blk
    n_items = item_blk.shape[0]
    tok3 = row_tok.reshape(n_blocks, 1, blk)
    kern = functools.partial(_moe_kernel, blk=blk)
    last = n_items - 1
    grid_spec = pltpu.PrefetchScalarGridSpec(
        num_scalar_prefetch=4,
        grid=(n_items,),
        in_specs=[pl.BlockSpec((None, 1, blk), lambda w, ib, ie, lo, hi: (ib[w], 0, 0), memory_space=pltpu.SMEM),
                  pl.BlockSpec((None, 1, blk), lambda w, ib, ie, lo, hi: (ib[jnp.minimum(w + 1, last)], 0, 0),
                               memory_space=pltpu.SMEM),
                  pl.BlockSpec(memory_space=pl.ANY),
                  pl.BlockSpec((None, D_MODEL, 2 * D_FF), lambda w, ib, ie, lo, hi: (ie[w], 0, 0)),
                  pl.BlockSpec((None, 1, 2 * D_FF), lambda w, ib, ie, lo, hi: (ie[w], 0, 0)),
                  pl.BlockSpec((None, D_FF, D_MODEL), lambda w, ib, ie, lo, hi: (ie[w], 0, 0)),
                  pl.BlockSpec((None, 1, D_MODEL), lambda w, ib, ie, lo, hi: (ie[w], 0, 0))],
        out_specs=pl.BlockSpec((blk, D_MODEL), lambda w, ib, ie, lo, hi: (ib[w], 0)),
        scratch_shapes=[pltpu.VMEM((blk, D_MODEL), F32), pltpu.VMEM((blk, D_MODEL), F32),
                        pltpu.SemaphoreType.DMA((2,))])
    return pl.pallas_call(
        kern, grid_spec=grid_spec,
        out_shape=jax.ShapeDtypeStruct((n_rows, D_MODEL), F32),
        compiler_params=_cparams(("arbitrary",)),
        name="routed_experts",
    )(item_blk, item_exp, item_lo, item_hi, tok3, tok3, h2, w_up,
      b_up.reshape(N_EXPERTS, 1, 2 * D_FF), w_down, b_down.reshape(N_EXPERTS, 1, D_MODEL))


def _route_plan(ridx, counts, blk):
    n_tok = ridx.shape[0]
    top_i, rank = ridx[:, :TOP_K], ridx[:, TOP_K:]
    counts = counts.reshape(N_EXPERTS).astype(jnp.int32)
    ends = jnp.cumsum(counts)
    starts = ends - counts
    experts = jnp.arange(N_EXPERTS, dtype=jnp.int32)
    start_of = jnp.sum(jnp.where(top_i[..., None] == experts, starts, 0), axis=-1)
    pos = (start_of + rank).astype(jnp.int32)
    t_flat = jnp.repeat(jnp.arange(n_tok, dtype=jnp.int32), TOP_K)
    _, row_tok = lax.sort_key_val(pos.reshape(-1), t_flat)
    n_blocks = n_tok * TOP_K // blk
    n_items = n_blocks + N_EXPERTS - 1
    first_blk = starts // blk
    n_blk = jnp.where(counts > 0, (ends - 1) // blk - first_blk + 1, 0)
    item_end = jnp.cumsum(n_blk)
    item_start = item_end - n_blk
    w = jnp.arange(n_items, dtype=jnp.int32)
    used = w < item_end[-1]
    e_w = jnp.minimum(jnp.searchsorted(item_end, w, side='right'), N_EXPERTS - 1).astype(jnp.int32)
    e_last = jnp.max(jnp.where(counts > 0, experts, 0))
    item_blk = jnp.where(used, first_blk[e_w] + w - item_start[e_w], n_blocks - 1).astype(jnp.int32)
    item_exp = jnp.where(used, e_w, e_last).astype(jnp.int32)
    item_lo = jnp.where(used, starts[e_w], 0).astype(jnp.int32)
    item_hi = jnp.where(used, ends[e_w], 0).astype(jnp.int32)
    return pos, row_tok, (item_blk, item_exp, item_lo, item_hi)


def _final_kernel(pos_ref, pos_next_ref, x1_ref, gate2_ref, rgate_ref, g_ref, rows_hbm, o_ref,
                  gbuf0_ref, gbuf1_ref, sem_ref, *, ts):
    i = pl.program_id(0)
    last = pl.num_programs(0) - 1
    n_rows = ts * TOP_K

    @pl.when(i == 0)
    def _():
        _start_rows(rows_hbm, pos_ref, gbuf0_ref, sem_ref.at[0], n_rows)

    def step(cur_ref, cur_sem, nxt_ref, nxt_sem):
        _wait_rows(rows_hbm, cur_ref, cur_sem, n_rows)
        _start_rows(rows_hbm, pos_next_ref, nxt_ref, nxt_sem, n_rows)
        rg = rgate_ref[...]
        ffn = rg[:, 0:1] * cur_ref[0:ts, :]
        for k in range(1, TOP_K):
            ffn = ffn + rg[:, k:k + 1] * cur_ref[k * ts:(k + 1) * ts, :]
        x = x1_ref[...] + gate2_ref[...] * ffn
        o_ref[...] = _rms(x, g_ref[...])

        @pl.when(i == last)
        def _():
            _wait_rows(rows_hbm, nxt_ref, nxt_sem, n_rows)

    @pl.when(i % 2 == 0)
    def _():
        step(gbuf0_ref, sem_ref.at[0], gbuf1_ref, sem_ref.at[1])

    @pl.when(i % 2 == 1)
    def _():
        step(gbuf1_ref, sem_ref.at[1], gbuf0_ref, sem_ref.at[0])


def _combine_final(x1, gate2, rgate, final_g, rows, pos, ts, per_token_mod):
    nb, s, _ = x1.shape
    ns = s // ts
    n_tiles = nb * ns
    pos3 = pos.reshape(n_tiles, ts, TOP_K).transpose(0, 2, 1).reshape(n_tiles, 1, ts * TOP_K)
    if per_token_mod:
        mod_spec = pl.BlockSpec((ts, D_MODEL), lambda i: (i, 0))
    else:
        mod_spec = pl.BlockSpec((None, 1, D_MODEL), lambda i: (i // ns, 0, 0))
    kern = functools.partial(_final_kernel, ts=ts)
    last = n_tiles - 1
    tok = lambda w: pl.BlockSpec((None, ts, w), lambda i: (i // ns, i % ns, 0))
    return pl.pallas_call(
        kern, grid=(n_tiles,),
        in_specs=[pl.BlockSpec((None, 1, ts * TOP_K), lambda i: (i, 0, 0), memory_space=pltpu.SMEM),
                  pl.BlockSpec((None, 1, ts * TOP_K), lambda i: (jnp.minimum(i + 1, last), 0, 0),
                               memory_space=pltpu.SMEM),
                  tok(D_MODEL), mod_spec, tok(TOP_K),
                  pl.BlockSpec((1, D_MODEL), lambda i: (0, 0)),
                  pl.BlockSpec(memory_space=pl.ANY)],
        out_specs=tok(D_MODEL),
        out_shape=jax.ShapeDtypeStruct((nb, s, D_MODEL), F32),
        scratch_shapes=[pltpu.VMEM((ts * TOP_K, D_MODEL), F32), pltpu.VMEM((ts * TOP_K, D_MODEL), F32),
                        pltpu.SemaphoreType.DMA((2,))],
        compiler_params=_cparams(("arbitrary",)),
        name="combine_final",
    )(pos3, pos3, x1, gate2, rgate, final_g, rows)


def _group_forward(x, mod, pos, h0_re, h0_im, past, lam, wts, *, per_token_mod, ts, seq_major_tokens):
    nb, s, _ = x.shape
    shift1, scale1, gate1, shift2, scale2, gate2 = mod
    u_tm, q, kt, v, gate = _in_projection(x, shift1, scale1, wts['norm1_g'], wts['w_in_parts'],
                                          _rope_tables(pos), ts, per_token_mod)
    if past is None:
        y_attn = _prompt_attention(lam, q, kt, v, wts['subln_g'], t=min(512, s))
        ssm_nb, ssm_len = nb, s
        u_scan = u_tm.reshape(s * nb, SSM_WIDTH)
    else:
        n_seq, seq_len = seq_major_tokens
        page_table, cache_kt, cache_v2 = past
        q_s = q.reshape(n_seq, seq_len, ATTN_W)
        kt_s = kt.reshape(ATTN_W, n_seq, seq_len).transpose(1, 0, 2)
        v_s = v.reshape(n_seq, seq_len, ATTN_W)
        y_attn = _sample_attention(page_table, lam, q_s, kt_s, v_s, wts['subln_g'], cache_kt, cache_v2,
                                   pages=min(16, page_table.shape[1])).reshape(nb, s, ATTN_W)
        ssm_nb, ssm_len = n_seq, seq_len
        u_scan = u_tm.reshape(n_seq, seq_len, SSM_WIDTH).transpose(1, 0, 2).reshape(s, SSM_WIDTH)
    y_tm, hT_re, hT_im = _ssm_scan(u_scan, h0_re, h0_im, wts['ssm_params'], wts['w_glu'], wts['b_glu'],
                                   nb=ssm_nb, steps=min(ssm_len, max(8, 512 // ssm_nb)))
    if past is None:
        y_ssm = y_tm.reshape(s, nb * SSM_WIDTH)
    else:
        y_ssm = y_tm.reshape(seq_len, n_seq, SSM_WIDTH).transpose(1, 0, 2).reshape(s, SSM_WIDTH)
    x1, h2, ridx, rgate, counts = _post_mixer(y_ssm, y_attn, gate, x, gate1, shift2, scale2, wts['post'], ts,
                                               per_token_mod)
    n_tok = nb * s
    blk = min(256, n_tok * TOP_K)
    row_pos, row_tok, plan = _route_plan(ridx.reshape(n_tok, 2 * TOP_K), counts, blk)
    rows = _routed_experts(h2.reshape(n_tok, D_MODEL), row_tok, plan,
                           wts['w_up'], wts['b_up'], wts['w_down'], wts['b_down'], blk)
    y = _combine_final(x1, gate2, rgate, wts['final_g'], rows, row_pos, min(128, s), per_token_mod)
    return y, kt, v, hT_re, hT_im


def kernel(x_prompt, x_sample, c_prompt, c_sample, cache_k, cache_v, state_ssm_re, state_ssm_im, page_table, norm1_g, w_ada, b_ada, w_in, ssm_a_re, ssm_a_im, ssm_log_dt, ssm_b_re, ssm_b_im, ssm_c_re, ssm_c_im, ssm_d, w_glu, b_glu, lambda_q1, lambda_k1, lambda_q2, lambda_k2, subln_g, p_ssm, p_attn, w_o, norm2_g, w_router, b_router, w_up, b_up, w_down, b_down, final_g):
    bsz_p, seq_p, _ = x_prompt.shape
    bsz_s, seq_s, _ = x_sample.shape
    n_pool, page = cache_k.shape[1], cache_k.shape[2]
    past_len = page_table.shape[1] * page
    l = 0

    w = w_in[l]
    c0 = SSM_WIDTH
    w_in_parts = (w[:, :c0].astype(BF16), w[:, c0:c0 + ATTN_W].astype(BF16),
                  w[:, c0 + ATTN_W:c0 + 2 * ATTN_W].T.astype(BF16),
                  w[:, c0 + 2 * ATTN_W:c0 + 3 * ATTN_W].astype(BF16), w[:, c0 + 3 * ATTN_W:].astype(BF16))
    wts = {
        'norm1_g': norm1_g[l].reshape(1, D_MODEL),
        'w_in_parts': w_in_parts,
        'subln_g': subln_g[l].reshape(1, HEAD_W),
        'ssm_params': _ssm_params(ssm_a_re[l], ssm_a_im[l], ssm_log_dt[l], ssm_b_re[l], ssm_b_im[l],
                                  ssm_c_re[l], ssm_c_im[l], ssm_d[l]),
        'w_glu': w_glu[l].astype(BF16), 'b_glu': b_glu[l].reshape(1, SSM_WIDTH),
        'post': (p_ssm[l].astype(BF16), p_attn[l].astype(BF16), w_o[l].astype(BF16),
                 norm2_g[l].reshape(1, D_MODEL), w_router[l], b_router[l].reshape(1, N_EXPERTS)),
        'w_up': w_up[l].astype(BF16), 'b_up': b_up[l], 'w_down': w_down[l].astype(BF16), 'b_down': b_down[l],
        'final_g': final_g.reshape(1, D_MODEL),
    }
    lam = (jnp.exp(jnp.sum(lambda_q1[l].astype(F32) * lambda_k1[l].astype(F32)))
           - jnp.exp(jnp.sum(lambda_q2[l].astype(F32) * lambda_k2[l].astype(F32))) + LAM_INIT).reshape(1)

    mod = _modulation(jnp.concatenate([c_prompt, c_sample], axis=0), w_ada[l], b_ada[l])
    mod_p = [m.reshape(bsz_p, 1, D_MODEL) for m in jnp.split(mod[:bsz_p], 6, axis=-1)]
    mod_s = [jnp.repeat(m, seq_s, axis=0) for m in jnp.split(mod[bsz_p:], 6, axis=-1)]

    zeros_state = jnp.zeros((bsz_p, SSM_LANES), F32)
    yp, ktp, vp, rp, ip = _group_forward(
        x_prompt, mod_p, jnp.arange(seq_p), zeros_state, zeros_state, None, lam, wts,
        per_token_mod=False, ts=min(256, seq_p), seq_major_tokens=None)

    cache_kt = jnp.transpose(cache_k[l], (0, 2, 3, 1)).reshape(n_pool, ATTN_W, page)
    cache_v2 = cache_v[l].reshape(n_pool, page * N_HEADS, HEAD_W)
    n_s = bsz_s * seq_s
    pos_s = past_len + jnp.tile(jnp.arange(seq_s), bsz_s)
    ys, kts, vs, rs, is_ = _group_forward(
        x_sample.reshape(1, n_s, D_MODEL), mod_s, pos_s,
        state_ssm_re[l].reshape(bsz_s, SSM_LANES), state_ssm_im[l].reshape(bsz_s, SSM_LANES),
        (page_table, cache_kt, cache_v2), lam, wts,
        per_token_mod=True, ts=n_s, seq_major_tokens=(bsz_s, seq_s))

    k_prompt = jnp.transpose(ktp.reshape(bsz_p, 2 * N_HEADS, HEAD_DIM, seq_p), (0, 3, 1, 2))[None]
    v_prompt = vp.reshape(bsz_p, seq_p, N_HEADS, HEAD_W)[None]
    k_sample = jnp.transpose(kts.reshape(2 * N_HEADS, HEAD_DIM, bsz_s, seq_s), (2, 3, 0, 1))[None]
    v_sample = vs.reshape(bsz_s, seq_s, N_HEADS, HEAD_W)[None]
    state = lambda h, n: h.reshape(n, SSM_GROUPS, SSM_STATE)[None]
    return (yp, ys.reshape(bsz_s, seq_s, D_MODEL), k_prompt, v_prompt, state(rp, bsz_p), state(ip, bsz_p),
            k_sample, v_sample, state(rs, bsz_s), state(is_, bsz_s))
```

```python
import functools
import math

import jax
import jax.numpy as jnp
from jax import lax
from jax.experimental import pallas as pl
from jax.experimental.pallas import tpu as pltpu

D_MODEL = 1024
SSM_WIDTH = D_MODEL // 2
SSM_GROUP = 16
SSM_GROUPS = SSM_WIDTH // SSM_GROUP
SSM_STATE = 64
SSM_LANES = SSM_GROUPS * SSM_STATE
MIN_DECAY = 1e-4
N_HEADS = 4
HEAD_DIM = 64
HEAD_W = 2 * HEAD_DIM
ATTN_W = N_HEADS * HEAD_W
ROT_DIM = HEAD_DIM // 4
ROT_HALF = ROT_DIM // 2
ROPE_THETA = 500000.0
NEG_INF = -1e30
N_EXPERTS = 32
TOP_K = 4
D_FF = D_MODEL
SWIGLU_LIMIT = 7.0
SWIGLU_ALPHA = 1.702
EPS = 1e-6
LAM_INIT = 0.8 - 0.6 * math.exp(-0.3 * 0)
LOG2_E = 1.4426950408889634

VMEM_LIMIT_BYTES = 56 * 1024 * 1024
LANE = 128
SUBLANES = 8
ROW_CHUNKS = D_MODEL // LANE
assert ROW_CHUNKS == SUBLANES

BF16 = jnp.bfloat16
F32 = jnp.float32


def _cparams(sem):
    return pltpu.CompilerParams(dimension_semantics=sem, vmem_limit_bytes=VMEM_LIMIT_BYTES)


def _dot(a, b):
    return jnp.dot(a, b, preferred_element_type=F32)


def _rms(x, g):
    return x * lax.rsqrt(jnp.mean(x * x, axis=-1, keepdims=True) + EPS) * g


def _mod_kernel(c_ref, w_ref, b_ref, o_ref):
    c = c_ref[...]
    s = (c * jax.nn.sigmoid(c)).astype(BF16)
    o_ref[...] = _dot(s, w_ref[...].astype(BF16)) + b_ref[...]


def _modulation(c, w_ada, b_ada):
    n = c.shape[0]
    return pl.pallas_call(
        _mod_kernel,
        grid=(6,),
        in_specs=[pl.BlockSpec((n, D_MODEL), lambda j: (0, 0)),
                  pl.BlockSpec((D_MODEL, D_MODEL), lambda j: (0, j)),
                  pl.BlockSpec((1, D_MODEL), lambda j: (0, j))],
        out_specs=pl.BlockSpec((n, D_MODEL), lambda j: (0, j)),
        out_shape=jax.ShapeDtypeStruct((n, 6 * D_MODEL), F32),
        compiler_params=_cparams(("arbitrary",)),
        name="adaln_mod",
    )(c, w_ada, b_ada.reshape(1, 6 * D_MODEL))


def _inproj_kernel(x_ref, shift_ref, scale_ref, g_ref, wu_ref, wq_ref, wkt_ref, wv_ref, wg_ref,
                   cosq_ref, sina_ref, sinb_ref, cost_ref, sint_ref,
                   u_ref, q_ref, kt_ref, v_ref, gate_ref):
    x = x_ref[...]
    h = _rms(x, g_ref[...]) * (1.0 + scale_ref[...]) + shift_ref[...]
    hb = h.astype(BF16)
    u_ref[...] = _dot(hb, wu_ref[...])
    v = _dot(hb, wv_ref[...])
    for hd in range(N_HEADS):
        v_ref[pl.ds(hd, x.shape[0], stride=N_HEADS), :] = v[:, hd * HEAD_W:(hd + 1) * HEAD_W]
    gate_ref[...] = _dot(hb, wg_ref[...])
    q = _dot(hb, wq_ref[...])
    blocks = [q[:, j:j + LANE] for j in range(0, ATTN_W, LANE)]
    q_up = jnp.concatenate([pltpu.roll(b, LANE - ROT_HALF, 1) for b in blocks], axis=1)
    q_dn = jnp.concatenate([pltpu.roll(b, ROT_HALF, 1) for b in blocks], axis=1)
    q = q * cosq_ref[...] + q_up * sina_ref[...] + q_dn * sinb_ref[...]
    q_ref[...] = (q * (HEAD_DIM ** -0.5 * LOG2_E)).astype(BF16)
    kt = lax.dot_general(wkt_ref[...], hb, (((1,), (1,)), ((), ())), preferred_element_type=F32)
    cos_t = cost_ref[...]
    sin_t = sint_ref[...]
    for hh in range(2 * N_HEADS):
        r0 = hh * HEAD_DIM
        x1 = kt[r0:r0 + ROT_HALF, :]
        x2 = kt[r0 + ROT_HALF:r0 + ROT_DIM, :]
        kt_ref[r0:r0 + ROT_HALF, :] = x1 * cos_t - x2 * sin_t
        kt_ref[r0 + ROT_HALF:r0 + ROT_DIM, :] = x2 * cos_t + x1 * sin_t
        kt_ref[r0 + ROT_DIM:r0 + HEAD_DIM, :] = kt[r0 + ROT_DIM:r0 + HEAD_DIM, :]


def _rope_tables(pos):
    inv_freq = ROPE_THETA ** (-jnp.arange(ROT_HALF, dtype=F32) * (2.0 / ROT_DIM))
    ang = pos.astype(F32)[:, None] * inv_freq[None, :]
    cos, sin = jnp.cos(ang), jnp.sin(ang)
    s = pos.shape[0]
    pad = HEAD_DIM - ROT_DIM
    cos_h = jnp.concatenate([cos, cos, jnp.ones((s, pad), F32)], axis=1)
    sina_h = jnp.concatenate([-sin, jnp.zeros((s, ROT_HALF + pad), F32)], axis=1)
    sinb_h = jnp.concatenate([jnp.zeros((s, ROT_HALF), F32), sin, jnp.zeros((s, pad), F32)], axis=1)
    reps = ATTN_W // HEAD_DIM
    return (jnp.tile(cos_h, (1, reps)), jnp.tile(sina_h, (1, reps)), jnp.tile(sinb_h, (1, reps)),
            cos.T, sin.T)


def _in_projection(x, shift, scale, g, weights, tables, ts, per_token_mod):
    nb, s, _ = x.shape
    ns = s // ts
    wu, wq, wkt, wv, wg = weights
    cosq, sina, sinb, cost, sint = tables
    if per_token_mod:
        mod_spec = pl.BlockSpec((ts, D_MODEL), lambda b, i: (b * ns + i, 0))
    else:
        mod_spec = pl.BlockSpec((None, 1, D_MODEL), lambda b, i: (b, 0, 0))
    const = lambda shape: pl.BlockSpec(shape, lambda b, i: (0,) * len(shape))
    tab = pl.BlockSpec((ts, ATTN_W), lambda b, i: (i, 0))
    tab_t = pl.BlockSpec((ROT_HALF, ts), lambda b, i: (0, i))
    return pl.pallas_call(
        _inproj_kernel,
        grid=(nb, ns),
        in_specs=[pl.BlockSpec((None, ts, D_MODEL), lambda b, i: (b, i, 0)), mod_spec, mod_spec,
                  const((1, D_MODEL)),
                  const((D_MODEL, SSM_WIDTH)), const((D_MODEL, ATTN_W)), const((ATTN_W, D_MODEL)),
                  const((D_MODEL, ATTN_W)), const((D_MODEL, 2 * D_MODEL)),
                  tab, tab, tab, tab_t, tab_t],
        out_specs=[pl.BlockSpec((ts, SSM_WIDTH), lambda b, i: (i, b)),
                   pl.BlockSpec((None, ts, ATTN_W), lambda b, i: (b, i, 0)),
                   pl.BlockSpec((None, ATTN_W, ts), lambda b, i: (b, 0, i)),
                   pl.BlockSpec((None, ts * N_HEADS, HEAD_W), lambda b, i: (b, i, 0)),
                   pl.BlockSpec((None, ts, 2 * D_MODEL), lambda b, i: (b, i, 0))],
        out_shape=[jax.ShapeDtypeStruct((s, nb * SSM_WIDTH), F32),
                   jax.ShapeDtypeStruct((nb, s, ATTN_W), BF16),
                   jax.ShapeDtypeStruct((nb, ATTN_W, s), F32),
                   jax.ShapeDtypeStruct((nb, s * N_HEADS, HEAD_W), F32),
                   jax.ShapeDtypeStruct((nb, s, 2 * D_MODEL), F32)],
        compiler_params=_cparams(("arbitrary", "arbitrary")),
        name="in_proj",
    )(x, shift, scale, g, wu, wq, wkt, wv, wg, cosq, sina, sinb, cost, sint)


def _softmax_step(s, v_bf, m_prev, l_prev, acc_prev):
    m_next = jnp.maximum(m_prev, jnp.max(s, axis=1, keepdims=True))
    p = jnp.exp2(s - jnp.tile(m_next, (1, s.shape[1] // LANE)))
    alpha = jnp.exp2(m_prev - m_next)
    l_next = alpha * l_prev + jnp.sum(p, axis=1, keepdims=True)
    acc_next = alpha * acc_prev + _dot(p.astype(BF16), v_bf)
    return m_next, l_next, acc_next


def _diff_head_out(lam, acc0, l0, acc1, l1, g):
    o = acc0 / l0 - lam * (acc1 / l1)
    return _rms(o, g) * (1.0 - LAM_INIT)


def _init_softmax_state(m_ref, l_ref, acc_ref):
    m_ref[...] = jnp.full(m_ref.shape, NEG_INF, F32)
    l_ref[...] = jnp.zeros(l_ref.shape, F32)
    acc_ref[...] = jnp.zeros(acc_ref.shape, F32)


def _prompt_attn_kernel(qi_ref, ki_ref, lam_ref, q_ref, kt_ref, v_ref, g_ref, o_ref, m_ref, l_ref, acc_ref, *, t):
    pair = pl.program_id(1)
    qi = qi_ref[pair]
    ki = ki_ref[pair]

    @pl.when(ki == 0)
    def _():
        _init_softmax_state(m_ref, l_ref, acc_ref)

    def update(masked):
        if masked:
            visible = (lax.broadcasted_iota(jnp.int32, (t, t), 1) <= lax.broadcasted_iota(jnp.int32, (t, t), 0))
        for h in range(N_HEADS):
            v_bf = v_ref[pl.ds(h, t, stride=N_HEADS), :].astype(BF16)
            for c in range(2):
                hc = 2 * h + c
                r0 = hc * HEAD_DIM
                s = _dot(q_ref[:, r0:r0 + HEAD_DIM], kt_ref[r0:r0 + HEAD_DIM, :].astype(BF16))
                if masked:
                    s = jnp.where(visible, s, NEG_INF)
                m_ref[hc], l_ref[hc], acc_ref[hc] = _softmax_step(s, v_bf, m_ref[hc], l_ref[hc], acc_ref[hc])

    @pl.when(ki < qi)
    def _():
        update(False)

    @pl.when(ki == qi)
    def _():
        update(True)
        lam = lam_ref[0]
        g = g_ref[...]
        for h in range(N_HEADS):
            o_ref[:, h * HEAD_W:(h + 1) * HEAD_W] = _diff_head_out(
                lam, acc_ref[2 * h], l_ref[2 * h], acc_ref[2 * h + 1], l_ref[2 * h + 1], g)


def _prompt_attention(lam, q, kt, v, subln_g, t):
    b, s, _ = q.shape
    n = s // t
    pairs = [(qi, ki) for qi in range(n) for ki in range(qi + 1)]
    qi_tab = jnp.asarray([p[0] for p in pairs], jnp.int32)
    ki_tab = jnp.asarray([p[1] for p in pairs], jnp.int32)
    kern = functools.partial(_prompt_attn_kernel, t=t)
    grid_spec = pltpu.PrefetchScalarGridSpec(
        num_scalar_prefetch=3,
        grid=(b, len(pairs)),
        in_specs=[pl.BlockSpec((None, t, ATTN_W), lambda bi, p, qt, kt_, lam_: (bi, qt[p], 0)),
                  pl.BlockSpec((None, ATTN_W, t), lambda bi, p, qt, kt_, lam_: (bi, 0, kt_[p])),
                  pl.BlockSpec((None, t * N_HEADS, HEAD_W), lambda bi, p, qt, kt_, lam_: (bi, kt_[p], 0)),
                  pl.BlockSpec((1, HEAD_W), lambda bi, p, qt, kt_, lam_: (0, 0))],
        out_specs=pl.BlockSpec((None, t, ATTN_W), lambda bi, p, qt, kt_, lam_: (bi, qt[p], 0)),
        scratch_shapes=[pltpu.VMEM((2 * N_HEADS, t, LANE), F32), pltpu.VMEM((2 * N_HEADS, t, LANE), F32),
                        pltpu.VMEM((2 * N_HEADS, t, HEAD_W), F32)])
    return pl.pallas_call(
        kern, grid_spec=grid_spec,
        out_shape=jax.ShapeDtypeStruct((b, s, ATTN_W), F32),
        compiler_params=_cparams(("arbitrary", "arbitrary")),
        name="prompt_attn",
    )(qi_tab, ki_tab, lam, q, kt, v, subln_g)


def _sample_attn_kernel(pt_ref, lam_ref, q_ref, ktn_ref, vn_ref, g_ref, *rest, pages, page):
    k_refs = rest[:pages]
    v_refs = rest[pages:2 * pages]
    o_ref, kcat_ref, vcat_ref, m_ref, l_ref, acc_ref = rest[2 * pages:]
    j = pl.program_id(1)
    nq = q_ref.shape[0]
    hrows = 2 * nq

    @pl.when(j == 0)
    def _():
        _init_softmax_state(m_ref, l_ref, acc_ref)

    for i in range(pages):
        kcat_ref[:, i * page:(i + 1) * page] = k_refs[i][...].astype(BF16)
        for h in range(N_HEADS):
            vcat_ref[h, i * page:(i + 1) * page, :] = v_refs[i][pl.ds(h, page, stride=N_HEADS), :].astype(BF16)

    q = q_ref[...]
    q_parts = [q[:, hc * HEAD_DIM:(hc + 1) * HEAD_DIM] for hc in range(2 * N_HEADS)]
    s = jnp.concatenate([_dot(q_parts[hc], kcat_ref[hc * HEAD_DIM:(hc + 1) * HEAD_DIM, :])
                         for hc in range(2 * N_HEADS)], axis=0)
    m_prev = m_ref[...]
    m_next = jnp.maximum(m_prev, jnp.max(s, axis=1, keepdims=True))
    p = jnp.exp2(s - jnp.tile(m_next, (1, s.shape[1] // LANE)))
    alpha = jnp.exp2(m_prev - m_next)
    l_ref[...] = alpha * l_ref[...] + jnp.sum(p, axis=1, keepdims=True)
    pb = p.astype(BF16)
    pv = jnp.concatenate([_dot(pb[h * hrows:(h + 1) * hrows], vcat_ref[h]) for h in range(N_HEADS)], axis=0)
    acc_ref[...] = alpha * acc_ref[...] + pv
    m_ref[...] = m_next

    @pl.when(j == pl.num_programs(1) - 1)
    def _():
        rows = 2 * N_HEADS * nq
        s_new = jnp.concatenate([_dot(q_parts[hc], ktn_ref[hc * HEAD_DIM:(hc + 1) * HEAD_DIM, :].astype(BF16))
                                 for hc in range(2 * N_HEADS)], axis=0)
        qpos = lax.broadcasted_iota(jnp.int32, (rows, nq), 0) % nq
        kpos = lax.broadcasted_iota(jnp.int32, (rows, nq), 1)
        s_new = jnp.where(kpos <= qpos, s_new, NEG_INF)
        m_old = m_ref[...]
        m_fin = jnp.maximum(m_old, jnp.max(s_new, axis=1, keepdims=True))
        p_new = jnp.exp2(s_new - m_fin[:, :nq])
        a_fin = jnp.exp2(m_old - m_fin)
        l_fin = a_fin * l_ref[...] + jnp.sum(p_new, axis=1, keepdims=True)
        pnb = p_new.astype(BF16)
        pv_new = jnp.concatenate(
            [_dot(pnb[h * hrows:(h + 1) * hrows], vn_ref[:, h * HEAD_W:(h + 1) * HEAD_W].astype(BF16))
             for h in range(N_HEADS)], axis=0)
        acc_fin = a_fin * acc_ref[...] + pv_new
        lam = lam_ref[0]
        g = g_ref[...]
        for h in range(N_HEADS):
            r0 = h * hrows
            o_ref[:, h * HEAD_W:(h + 1) * HEAD_W] = _diff_head_out(
                lam, acc_fin[r0:r0 + nq], l_fin[r0:r0 + nq], acc_fin[r0 + nq:r0 + hrows], l_fin[r0 + nq:r0 + hrows], g)


def _sample_attention(page_table, lam, q, kt_new, v_new, subln_g, cache_kt, cache_v2, pages):
    b, nq, _ = q.shape
    n_pages = page_table.shape[1]
    page = cache_kt.shape[2]
    nj = n_pages // pages
    rows = 2 * N_HEADS * nq
    kern = functools.partial(_sample_attn_kernel, pages=pages, page=page)

    def k_spec(i):
        return pl.BlockSpec((None, ATTN_W, page), lambda bi, j, pt, lam_: (pt[bi, j * pages + i], 0, 0))

    def v_spec(i):
        return pl.BlockSpec((None, page * N_HEADS, HEAD_W), lambda bi, j, pt, lam_: (pt[bi, j * pages + i], 0, 0))

    grid_spec = pltpu.PrefetchScalarGridSpec(
        num_scalar_prefetch=2,
        grid=(b, nj),
        in_specs=[pl.BlockSpec((None, nq, ATTN_W), lambda bi, j, pt, lam_: (bi, 0, 0)),
                  pl.BlockSpec((None, ATTN_W, nq), lambda bi, j, pt, lam_: (bi, 0, 0)),
                  pl.BlockSpec((None, nq, ATTN_W), lambda bi, j, pt, lam_: (bi, 0, 0)),
                  pl.BlockSpec((1, HEAD_W), lambda bi, j, pt, lam_: (0, 0))]
                 + [k_spec(i) for i in range(pages)] + [v_spec(i) for i in range(pages)],
        out_specs=pl.BlockSpec((None, nq, ATTN_W), lambda bi, j, pt, lam_: (bi, 0, 0)),
        scratch_shapes=[pltpu.VMEM((ATTN_W, pages * page), BF16),
                        pltpu.VMEM((N_HEADS, pages * page, HEAD_W), BF16),
                        pltpu.VMEM((rows, LANE), F32), pltpu.VMEM((rows, LANE), F32),
                        pltpu.VMEM((rows, HEAD_W), F32)])
    return pl.pallas_call(
        kern, grid_spec=grid_spec,
        out_shape=jax.ShapeDtypeStruct((b, nq, ATTN_W), F32),
        compiler_params=_cparams(("arbitrary", "arbitrary")),
        name="sample_attn",
    )(page_table, lam, q, kt_new, v_new, subln_g, *([cache_kt] * pages), *([cache_v2] * pages))


def _ssm_kernel(u_ref, h0r_ref, h0i_ref, ar_ref, ai_ref, bbr_ref, bbi_ref, ccr_ref, cci_ref, d_ref,
                wglu_ref, bglu_ref, y_ref, hr_out, hi_out, sr_ref, si_ref, br_ref, bi_ref, tm_ref,
                *, steps, nb, lane_chunk, batch_on_lanes):
    i = pl.program_id(0)

    @pl.when(i == 0)
    def _():
        sr_ref[...] = h0r_ref[...]
        si_ref[...] = h0i_ref[...]

    n_slabs = SSM_WIDTH // LANE
    if batch_on_lanes:
        for b in range(nb):
            for c in range(n_slabs):
                col = b * SSM_WIDTH + c * LANE
                tm_ref[c, pl.ds(b, steps, stride=nb), :] = u_ref[:, col:col + LANE]
        u = jnp.concatenate([tm_ref[c] for c in range(n_slabs)], axis=1)
    else:
        u = u_ref[...]
    ub = u.astype(BF16)
    br_ref[...] = _dot(ub, bbr_ref[...])
    bi_ref[...] = _dot(ub, bbi_ref[...])

    for c0 in range(0, SSM_LANES, lane_chunk):
        lanes = pl.ds(c0, lane_chunk)
        a_r = jnp.broadcast_to(ar_ref[:, lanes], (nb, lane_chunk))
        a_i = jnp.broadcast_to(ai_ref[:, lanes], (nb, lane_chunk))

        def step(t, carry):
            h_r, h_i = carry
            rows = pl.ds(pl.multiple_of(t * nb, nb), nb)
            n_r = a_r * h_r - a_i * h_i + br_ref[rows, lanes]
            n_i = a_r * h_i + a_i * h_r + bi_ref[rows, lanes]
            br_ref[rows, lanes] = n_r
            bi_ref[rows, lanes] = n_i
            return n_r, n_i

        h_r, h_i = lax.fori_loop(0, steps, step, (sr_ref[:, lanes], si_ref[:, lanes]), unroll=4)
        sr_ref[:, lanes] = h_r
        si_ref[:, lanes] = h_i

    y = (_dot(br_ref[...].astype(BF16), ccr_ref[...]) - _dot(bi_ref[...].astype(BF16), cci_ref[...])
         + d_ref[...] * u)
    y = jax.nn.gelu(y, approximate=True)
    y = y * jax.nn.sigmoid(_dot(y.astype(BF16), wglu_ref[...]) + bglu_ref[...])
    if batch_on_lanes:
        for c in range(n_slabs):
            tm_ref[c] = y[:, c * LANE:(c + 1) * LANE]
        for b in range(nb):
            for c in range(n_slabs):
                col = b * SSM_WIDTH + c * LANE
                y_ref[:, col:col + LANE] = tm_ref[c, pl.ds(b, steps, stride=nb), :]
    else:
        y_ref[...] = y

    @pl.when(i == pl.num_programs(0) - 1)
    def _():
        hr_out[...] = sr_ref[...]
        hi_out[...] = si_ref[...]


def _ssm_params(a_re, a_im, log_dt, b_re, b_im, c_re, c_im, d_skip):
    lam_re = jnp.minimum(a_re.astype(F32), -MIN_DECAY)
    lam_im = a_im.astype(F32)
    dt = jnp.exp(log_dt.astype(F32))[:, None]
    mag = jnp.exp(lam_re * dt)
    ab_re = mag * jnp.cos(lam_im * dt)
    ab_im = mag * jnp.sin(lam_im * dt)
    den = lam_re * lam_re + lam_im * lam_im
    n_re = ab_re - 1.0
    n_im = ab_im
    zoh_re = (n_re * lam_re + n_im * lam_im) / den
    zoh_im = (n_im * lam_re - n_re * lam_im) / den
    br, bi = b_re.astype(F32), b_im.astype(F32)
    bb_re = zoh_re[..., None] * br - zoh_im[..., None] * bi
    bb_im = zoh_re[..., None] * bi + zoh_im[..., None] * br
    eye = jnp.eye(SSM_GROUPS, dtype=F32)

    def in_blockdiag(bb):
        m = jnp.einsum('gpc,gh->gchp', bb, eye)
        return m.reshape(SSM_WIDTH, SSM_LANES).astype(BF16)

    def out_blockdiag(cc):
        m = jnp.einsum('gcp,gh->gphc', cc.astype(F32), eye)
        return m.reshape(SSM_LANES, SSM_WIDTH).astype(BF16)

    return (ab_re.reshape(1, SSM_LANES), ab_im.reshape(1, SSM_LANES), in_blockdiag(bb_re), in_blockdiag(bb_im),
            out_blockdiag(c_re), out_blockdiag(c_im), d_skip.astype(F32).reshape(1, SSM_WIDTH))


def _ssm_scan(u, h0_re, h0_im, params, w_glu, b_glu, nb, steps, batch_on_lanes):
    rows = steps * nb
    n_chunks = u.shape[0] * u.shape[1] // (rows * SSM_WIDTH)
    ab_re, ab_im, bbr, bbi, ccr, cci, d_row = params
    lane_chunk = min(SSM_LANES, 8 * 1024 // nb)
    kern = functools.partial(_ssm_kernel, steps=steps, nb=nb, lane_chunk=lane_chunk, batch_on_lanes=batch_on_lanes)
    const = lambda shape: pl.BlockSpec(shape, lambda i: (0,) * len(shape))
    io_block = (steps, nb * SSM_WIDTH) if batch_on_lanes else (rows, SSM_WIDTH)
    return pl.pallas_call(
        kern, grid=(n_chunks,),
        in_specs=[pl.BlockSpec(io_block, lambda i: (i, 0)),
                  const((nb, SSM_LANES)), const((nb, SSM_LANES)),
                  const((1, SSM_LANES)), const((1, SSM_LANES)),
                  const((SSM_WIDTH, SSM_LANES)), const((SSM_WIDTH, SSM_LANES)),
                  const((SSM_LANES, SSM_WIDTH)), const((SSM_LANES, SSM_WIDTH)),
                  const((1, SSM_WIDTH)), const((SSM_WIDTH, SSM_WIDTH)), const((1, SSM_WIDTH))],
        out_specs=[pl.BlockSpec(io_block, lambda i: (i, 0)),
                   const((nb, SSM_LANES)), const((nb, SSM_LANES))],
        out_shape=[jax.ShapeDtypeStruct(u.shape, F32),
                   jax.ShapeDtypeStruct((nb, SSM_LANES), F32), jax.ShapeDtypeStruct((nb, SSM_LANES), F32)],
        scratch_shapes=[pltpu.VMEM((nb, SSM_LANES), F32), pltpu.VMEM((nb, SSM_LANES), F32),
                        pltpu.VMEM((rows, SSM_LANES), F32), pltpu.VMEM((rows, SSM_LANES), F32),
                        pltpu.VMEM((SSM_WIDTH // LANE, rows, LANE), F32)],
        compiler_params=_cparams(("arbitrary",)),
        name="ssm_scan",
    )(u, h0_re, h0_im, ab_re, ab_im, bbr, bbi, ccr, cci, d_row, w_glu, b_glu)


def _post_kernel(yssm_ref, yattn_ref, gate_ref, x_ref, gate1_ref, shift2_ref, scale2_ref,
                 pssm_ref, pattn_ref, wo_ref, g2_ref, wr_ref, br_ref,
                 x1_ref, h2_ref, ridx_ref, rgate_ref, cnt_ref, carry_ref, *, ts):
    @pl.when(jnp.logical_and(pl.program_id(0) == 0, pl.program_id(1) == 0))
    def _():
        carry_ref[...] = jnp.zeros(carry_ref.shape, F32)

    gates = gate_ref[...]
    a = _dot(yssm_ref[...].astype(BF16), pssm_ref[...])
    b = _dot(yattn_ref[...].astype(BF16), pattn_ref[...])
    merged = jax.nn.sigmoid(gates[:, :D_MODEL]) * a + jax.nn.sigmoid(gates[:, D_MODEL:]) * b
    x1 = x_ref[...] + gate1_ref[...] * _dot(merged.astype(BF16), wo_ref[...])
    x1_ref[...] = x1
    h2 = _rms(x1, g2_ref[...]) * (1.0 + scale2_ref[...]) + shift2_ref[...]
    _store_tile_rows(h2_ref, h2)
    logits = jnp.dot(h2, wr_ref[...], preferred_element_type=F32,
                     precision=lax.Precision.HIGHEST) + br_ref[...]

    lane = lax.broadcasted_iota(jnp.int32, (ts, N_EXPERTS), 1)
    work = logits
    idxs, vals = [], []
    for _ in range(TOP_K):
        mk = jnp.max(work, axis=1, keepdims=True)
        ik = jnp.min(jnp.where(work == mk, lane, N_EXPERTS), axis=1, keepdims=True)
        idxs.append(ik)
        vals.append(mk)
        work = jnp.where(lane == ik, -jnp.inf, work)
    exps = [jnp.exp(v - vals[0]) for v in vals]
    den = exps[0] + exps[1] + exps[2] + exps[3]
    rgate_ref[...] = jnp.concatenate([e / den for e in exps], axis=1)

    chosen = jnp.zeros((ts, N_EXPERTS), F32)
    for ik in idxs:
        chosen = chosen + jnp.where(lane == ik, 1.0, 0.0)
    tri = jnp.where(lax.broadcasted_iota(jnp.int32, (ts, ts), 1) < lax.broadcasted_iota(jnp.int32, (ts, ts), 0),
                    1.0, 0.0).astype(BF16)
    before = _dot(tri, chosen.astype(BF16)) + carry_ref[...]
    ranks = [jnp.sum(jnp.where(lane == ik, before, 0.0), axis=1, keepdims=True).astype(jnp.int32) for ik in idxs]
    ridx_ref[...] = jnp.concatenate(idxs + ranks, axis=1)
    carry_ref[...] = carry_ref[...] + jnp.sum(chosen, axis=0, keepdims=True)
    cnt_ref[...] = carry_ref[...]


def _post_mixer(yssm_tm, yattn, gate, x, gate1, shift2, scale2, weights, ts, per_token_mod):
    nb, s, _ = x.shape
    ns = s // ts
    p_ssm, p_attn, w_o, g2, w_r, b_r = weights
    if per_token_mod:
        mod_spec = pl.BlockSpec((ts, D_MODEL), lambda b, i: (b * ns + i, 0))
    else:
        mod_spec = pl.BlockSpec((None, 1, D_MODEL), lambda b, i: (b, 0, 0))
    const = lambda shape: pl.BlockSpec(shape, lambda b, i: (0,) * len(shape))
    tok = lambda w: pl.BlockSpec((None, ts, w), lambda b, i: (b, i, 0))
    return pl.pallas_call(
        functools.partial(_post_kernel, ts=ts), grid=(nb, ns),
        in_specs=[pl.BlockSpec((ts, SSM_WIDTH), lambda b, i: (i, b)), tok(ATTN_W), tok(2 * D_MODEL), tok(D_MODEL),
                  mod_spec, mod_spec, mod_spec,
                  const((SSM_WIDTH, D_MODEL)), const((ATTN_W, D_MODEL)), const((D_MODEL, D_MODEL)),
                  const((1, D_MODEL)), const((D_MODEL, N_EXPERTS)), const((1, N_EXPERTS))],
        out_specs=[tok(D_MODEL), pl.BlockSpec((None, ts * ROW_CHUNKS, LANE), lambda b, i: (b, i, 0)),
                   tok(2 * TOP_K), tok(TOP_K), const((1, N_EXPERTS))],
        out_shape=[jax.ShapeDtypeStruct((nb, s, D_MODEL), F32),
                   jax.ShapeDtypeStruct((nb, s * ROW_CHUNKS, LANE), F32),
                   jax.ShapeDtypeStruct((nb, s, 2 * TOP_K), jnp.int32),
                   jax.ShapeDtypeStruct((nb, s, TOP_K), F32),
                   jax.ShapeDtypeStruct((1, N_EXPERTS), F32)],
        scratch_shapes=[pltpu.VMEM((1, N_EXPERTS), F32)],
        compiler_params=_cparams(("arbitrary", "arbitrary")),
        name="post_mixer",
    )(yssm_tm, yattn, gate, x, gate1, shift2, scale2, p_ssm, p_attn, w_o, g2, w_r, b_r)


def _store_tile_rows(ref, x, row0=0):
    n = x.shape[0]
    for j in range(ROW_CHUNKS):
        ref[pl.ds(row0 * ROW_CHUNKS + j, n, stride=ROW_CHUNKS), :] = x[:, j * LANE:(j + 1) * LANE]


def _load_tile_rows(ref, n, row0=0):
    return jnp.concatenate([ref[pl.ds(row0 * ROW_CHUNKS + j, n, stride=ROW_CHUNKS), :]
                            for j in range(ROW_CHUNKS)], axis=1)


def _row_copy(src_hbm, src_start, dst_ref, dst_row, sem):
    return pltpu.make_async_copy(src_hbm.at[pl.ds(src_start, ROW_CHUNKS)],
                                 dst_ref.at[pl.ds(dst_row * ROW_CHUNKS, ROW_CHUNKS)], sem)


def _start_rows(src_hbm, idx_ref, dst_ref, sem, n):
    for r in range(n):
        src_start = pl.multiple_of(idx_ref[0, r] * ROW_CHUNKS, ROW_CHUNKS)
        _row_copy(src_hbm, src_start, dst_ref, r, sem).start(priority=r % 2)


def _wait_rows(src_hbm, dst_ref, sem, n):
    for r in range(n):
        _row_copy(src_hbm, 0, dst_ref, r, sem).wait()


def _moe_kernel(iblk_ref, iexp_ref, ilo_ref, ihi_ref, tok_ref, tok_next_ref, h_hbm, wup_ref, bup_ref,
                wdn_ref, bdn_ref, o_ref, xbuf0_ref, xbuf1_ref, sem_ref, *, blk):
    w = pl.program_id(0)
    last = pl.num_programs(0) - 1

    @pl.when(w == 0)
    def _():
        _start_rows(h_hbm, tok_ref, xbuf0_ref, sem_ref.at[0], blk)

    def step(cur_ref, cur_sem, nxt_ref, nxt_sem):
        @pl.when(w <= last)
        def _():
            _start_rows(h_hbm, tok_next_ref, nxt_ref, nxt_sem, blk)

        _wait_rows(h_hbm, cur_ref, cur_sem, blk)
        xb = _load_tile_rows(cur_ref, blk).astype(BF16)
        gu = _dot(xb, wup_ref[...]) + bup_ref[...]
        g_lin = jnp.minimum(gu[:, :D_FF], SWIGLU_LIMIT)
        u_lin = jnp.clip(gu[:, D_FF:], -SWIGLU_LIMIT, SWIGLU_LIMIT)
        act = g_lin * jax.nn.sigmoid(SWIGLU_ALPHA * g_lin) * (u_lin + 1.0)
        res = _dot(act.astype(BF16), wdn_ref[...]) + bdn_ref[...]

        rows = iblk_ref[w] * blk + lax.broadcasted_iota(jnp.int32, (blk, 1), 0)
        mine = jnp.logical_and(rows >= ilo_ref[w], rows < ihi_ref[w])
        first = jnp.logical_or(w == 0, iblk_ref[w] != iblk_ref[jnp.maximum(w - 1, 0)])

        @pl.when(first)
        def _():
            _store_tile_rows(o_ref, jnp.where(mine, res, 0.0))

        @pl.when(jnp.logical_not(first))
        def _():
            _store_tile_rows(o_ref, jnp.where(mine, res, _load_tile_rows(o_ref, blk)))

        @pl.when(w == last)
        def _():
            _wait_rows(h_hbm, nxt_ref, nxt_sem, blk)

    @pl.when(w % 2 == 0)
    def _():
        step(xbuf0_ref, sem_ref.at[0], xbuf1_ref, sem_ref.at[1])

    @pl.when(w % 2 == 1)
    def _():
        step(xbuf1_ref, sem_ref.at[1], xbuf0_ref, sem_ref.at[0])


def _routed_experts(h2, row_tok, plan, w_up, b_up, w_down, b_down, blk):
    item_blk, item_exp, item_lo, item_hi = plan
    n_rows = row_tok.shape[0]
    n_blocks = n_rows // blk
    n_items = item_blk.shape[0]
    tok3 = row_tok.reshape(n_blocks, 1, blk)
    kern = functools.partial(_moe_kernel, blk=blk)
    last = n_items - 1
    grid_spec = pltpu.PrefetchScalarGridSpec(
        num_scalar_prefetch=4,
        grid=(n_items,),
        in_specs=[pl.BlockSpec((None, 1, blk), lambda w, ib, ie, lo, hi: (ib[w], 0, 0), memory_space=pltpu.SMEM),
                  pl.BlockSpec((None, 1, blk), lambda w, ib, ie, lo, hi: (ib[jnp.minimum(w + 1, last)], 0, 0),
                               memory_space=pltpu.SMEM),
                  pl.BlockSpec(memory_space=pl.ANY),
                  pl.BlockSpec((None, D_MODEL, 2 * D_FF), lambda w, ib, ie, lo, hi: (ie[w], 0, 0)),
                  pl.BlockSpec((None, 1, 2 * D_FF), lambda w, ib, ie, lo, hi: (ie[w], 0, 0)),
                  pl.BlockSpec((None, D_FF, D_MODEL), lambda w, ib, ie, lo, hi: (ie[w], 0, 0)),
                  pl.BlockSpec((None, 1, D_MODEL), lambda w, ib, ie, lo, hi: (ie[w], 0, 0))],
        out_specs=pl.BlockSpec((blk * ROW_CHUNKS, LANE), lambda w, ib, ie, lo, hi: (ib[w], 0)),
        scratch_shapes=[pltpu.VMEM((blk * ROW_CHUNKS, LANE), F32), pltpu.VMEM((blk * ROW_CHUNKS, LANE), F32),
                        pltpu.SemaphoreType.DMA((2,))])
    return pl.pallas_call(
        kern, grid_spec=grid_spec,
        out_shape=jax.ShapeDtypeStruct((n_rows * ROW_CHUNKS, LANE), F32),
        compiler_params=_cparams(("arbitrary",)),
        name="routed_experts",
    )(item_blk, item_exp, item_lo, item_hi, tok3, tok3, h2, w_up,
      b_up.reshape(N_EXPERTS, 1, 2 * D_FF), w_down, b_down.reshape(N_EXPERTS, 1, D_MODEL))


def _route_plan(ridx, counts, blk):
    n_tok = ridx.shape[0]
    top_i, rank = ridx[:, :TOP_K], ridx[:, TOP_K:]
    counts = counts.reshape(N_EXPERTS).astype(jnp.int32)
    ends = jnp.cumsum(counts)
    starts = ends - counts
    experts = jnp.arange(N_EXPERTS, dtype=jnp.int32)
    start_of = jnp.sum(jnp.where(top_i[..., None] == experts, starts, 0), axis=-1)
    pos = (start_of + rank).astype(jnp.int32)
    t_flat = jnp.repeat(jnp.arange(n_tok, dtype=jnp.int32), TOP_K)
    _, row_tok = lax.sort_key_val(pos.reshape(-1), t_flat)
    n_blocks = n_tok * TOP_K // blk
    n_items = n_blocks + N_EXPERTS - 1
    first_blk = starts // blk
    n_blk = jnp.where(counts > 0, (ends - 1) // blk - first_blk + 1, 0)
    item_end = jnp.cumsum(n_blk)
    item_start = item_end - n_blk
    w = jnp.arange(n_items, dtype=jnp.int32)
    used = w < item_end[-1]
    e_w = jnp.minimum(jnp.sum((item_end[None, :] <= w[:, None]).astype(jnp.int32), axis=1), N_EXPERTS - 1)
    e_last = jnp.max(jnp.where(counts > 0, experts, 0))
    sel = e_w[:, None] == experts[None, :]
    of_item = lambda per_expert: jnp.sum(jnp.where(sel, per_expert[None, :], 0), axis=1)
    item_blk = jnp.where(used, of_item(first_blk) + w - of_item(item_start), n_blocks - 1).astype(jnp.int32)
    item_exp = jnp.where(used, e_w, e_last).astype(jnp.int32)
    item_lo = jnp.where(used, of_item(starts), 0).astype(jnp.int32)
    item_hi = jnp.where(used, of_item(ends), 0).astype(jnp.int32)
    return pos, row_tok, (item_blk, item_exp, item_lo, item_hi)


def _final_kernel(pos_ref, pos_next_ref, x1_ref, gate2_ref, rgate_ref, g_ref, rows_hbm, o_ref,
                  gbuf0_ref, gbuf1_ref, sem_ref, *, ts):
    i = pl.program_id(0)
    last = pl.num_programs(0) - 1
    n_rows = ts * TOP_K

    @pl.when(i == 0)
    def _():
        _start_rows(rows_hbm, pos_ref, gbuf0_ref, sem_ref.at[0], n_rows)

    def step(cur_ref, cur_sem, nxt_ref, nxt_sem):
        _wait_rows(rows_hbm, cur_ref, cur_sem, n_rows)
        _start_rows(rows_hbm, pos_next_ref, nxt_ref, nxt_sem, n_rows)
        rg = rgate_ref[...]
        ffn = rg[:, 0:1] * _load_tile_rows(cur_ref, ts)
        for k in range(1, TOP_K):
            ffn = ffn + rg[:, k:k + 1] * _load_tile_rows(cur_ref, ts, row0=k * ts)
        x = x1_ref[...] + gate2_ref[...] * ffn
        o_ref[...] = _rms(x, g_ref[...])

        @pl.when(i == last)
        def _():
            _wait_rows(rows_hbm, nxt_ref, nxt_sem, n_rows)

    @pl.when(i % 2 == 0)
    def _():
        step(gbuf0_ref, sem_ref.at[0], gbuf1_ref, sem_ref.at[1])

    @pl.when(i % 2 == 1)
    def _():
        step(gbuf1_ref, sem_ref.at[1], gbuf0_ref, sem_ref.at[0])


def _combine_final(x1, gate2, rgate, final_g, rows, pos, ts, per_token_mod):
    nb, s, _ = x1.shape
    ns = s // ts
    n_tiles = nb * ns
    pos3 = pos.reshape(n_tiles, ts, TOP_K).transpose(0, 2, 1).reshape(n_tiles, 1, ts * TOP_K)
    if per_token_mod:
        mod_spec = pl.BlockSpec((ts, D_MODEL), lambda i: (i, 0))
    else:
        mod_spec = pl.BlockSpec((None, 1, D_MODEL), lambda i: (i // ns, 0, 0))
    kern = functools.partial(_final_kernel, ts=ts)
    last = n_tiles - 1
    tok = lambda w: pl.BlockSpec((None, ts, w), lambda i: (i // ns, i % ns, 0))
    return pl.pallas_call(
        kern, grid=(n_tiles,),
        in_specs=[pl.BlockSpec((None, 1, ts * TOP_K), lambda i: (i, 0, 0), memory_space=pltpu.SMEM),
                  pl.BlockSpec((None, 1, ts * TOP_K), lambda i: (jnp.minimum(i + 1, last), 0, 0),
                               memory_space=pltpu.SMEM),
                  tok(D_MODEL), mod_spec, tok(TOP_K),
                  pl.BlockSpec((1, D_MODEL), lambda i: (0, 0)),
                  pl.BlockSpec(memory_space=pl.ANY)],
        out_specs=tok(D_MODEL),
        out_shape=jax.ShapeDtypeStruct((nb, s, D_MODEL), F32),
        scratch_shapes=[pltpu.VMEM((ts * TOP_K * ROW_CHUNKS, LANE), F32),
                        pltpu.VMEM((ts * TOP_K * ROW_CHUNKS, LANE), F32), pltpu.SemaphoreType.DMA((2,))],
        compiler_params=_cparams(("arbitrary",)),
        name="combine_final",
    )(pos3, pos3, x1, gate2, rgate, final_g, rows)


def _group_forward(x, mod, pos, h0_re, h0_im, past, lam, wts, *, per_token_mod, ts, seq_major_tokens):
    nb, s, _ = x.shape
    shift1, scale1, gate1, shift2, scale2, gate2 = mod
    u_tm, q, kt, v, gate = _in_projection(x, shift1, scale1, wts['norm1_g'], wts['w_in_parts'],
                                          _rope_tables(pos), ts, per_token_mod)
    if past is None:
        y_attn = _prompt_attention(lam, q, kt, v, wts['subln_g'], t=min(512, s))
        ssm_nb, ssm_len = nb, s
        u_scan = u_tm
    else:
        n_seq, seq_len = seq_major_tokens
        page_table, cache_kt, cache_v2 = past
        q_s = q.reshape(n_seq, seq_len, ATTN_W)
        kt_s = kt.reshape(ATTN_W, n_seq, seq_len).transpose(1, 0, 2)
        v_s = v.reshape(n_seq, seq_len, ATTN_W)
        y_attn = _sample_attention(page_table, lam, q_s, kt_s, v_s, wts['subln_g'], cache_kt, cache_v2,
                                   pages=min(16, page_table.shape[1])).reshape(nb, s, ATTN_W)
        ssm_nb, ssm_len = n_seq, seq_len
        u_scan = u_tm.reshape(n_seq, seq_len, SSM_WIDTH).transpose(1, 0, 2).reshape(s, SSM_WIDTH)
    y_tm, hT_re, hT_im = _ssm_scan(u_scan, h0_re, h0_im, wts['ssm_params'], wts['w_glu'], wts['b_glu'],
                                   nb=ssm_nb, steps=min(ssm_len, max(8, 512 // ssm_nb)),
                                   batch_on_lanes=past is None)
    if past is None:
        y_ssm = y_tm
    else:
        y_ssm = y_tm.reshape(seq_len, n_seq, SSM_WIDTH).transpose(1, 0, 2).reshape(s, SSM_WIDTH)
    x1, h2, ridx, rgate, counts = _post_mixer(y_ssm, y_attn, gate, x, gate1, shift2, scale2, wts['post'], ts,
                                               per_token_mod)
    n_tok = nb * s
    blk = min(256, n_tok * TOP_K)
    row_pos, row_tok, plan = _route_plan(ridx.reshape(n_tok, 2 * TOP_K), counts, blk)
    rows = _routed_experts(h2.reshape(n_tok * ROW_CHUNKS, LANE), row_tok, plan,
                           wts['w_up'], wts['b_up'], wts['w_down'], wts['b_down'], blk)
    y = _combine_final(x1, gate2, rgate, wts['final_g'], rows, row_pos, min(128, s), per_token_mod)
    return y, kt, v, hT_re, hT_im


def kernel(x_prompt, x_sample, c_prompt, c_sample, cache_k, cache_v, state_ssm_re, state_ssm_im, page_table, norm1_g, w_ada, b_ada, w_in, ssm_a_re, ssm_a_im, ssm_log_dt, ssm_b_re, ssm_b_im, ssm_c_re, ssm_c_im, ssm_d, w_glu, b_glu, lambda_q1, lambda_k1, lambda_q2, lambda_k2, subln_g, p_ssm, p_attn, w_o, norm2_g, w_router, b_router, w_up, b_up, w_down, b_down, final_g):
    bsz_p, seq_p, _ = x_prompt.shape
    bsz_s, seq_s, _ = x_sample.shape
    n_pool, page = cache_k.shape[1], cache_k.shape[2]
    past_len = page_table.shape[1] * page
    l = 0

    w = w_in[l]
    c0 = SSM_WIDTH
    w_in_parts = (w[:, :c0].astype(BF16), w[:, c0:c0 + ATTN_W].astype(BF16),
                  w[:, c0 + ATTN_W:c0 + 2 * ATTN_W].T.astype(BF16),
                  w[:, c0 + 2 * ATTN_W:c0 + 3 * ATTN_W].astype(BF16), w[:, c0 + 3 * ATTN_W:].astype(BF16))
    wts = {
        'norm1_g': norm1_g[l].reshape(1, D_MODEL),
        'w_in_parts': w_in_parts,
        'subln_g': subln_g[l].reshape(1, HEAD_W),
        'ssm_params': _ssm_params(ssm_a_re[l], ssm_a_im[l], ssm_log_dt[l], ssm_b_re[l], ssm_b_im[l],
                                  ssm_c_re[l], ssm_c_im[l], ssm_d[l]),
        'w_glu': w_glu[l].astype(BF16), 'b_glu': b_glu[l].reshape(1, SSM_WIDTH),
        'post': (p_ssm[l].astype(BF16), p_attn[l].astype(BF16), w_o[l].astype(BF16),
                 norm2_g[l].reshape(1, D_MODEL), w_router[l], b_router[l].reshape(1, N_EXPERTS)),
        'w_up': w_up[l].astype(BF16), 'b_up': b_up[l], 'w_down': w_down[l].astype(BF16), 'b_down': b_down[l],
        'final_g': final_g.reshape(1, D_MODEL),
    }
    lam = (jnp.exp(jnp.sum(lambda_q1[l].astype(F32) * lambda_k1[l].astype(F32)))
           - jnp.exp(jnp.sum(lambda_q2[l].astype(F32) * lambda_k2[l].astype(F32))) + LAM_INIT).reshape(1)

    mod = _modulation(jnp.concatenate([c_prompt, c_sample], axis=0), w_ada[l], b_ada[l])
    mod_p = [m.reshape(bsz_p, 1, D_MODEL) for m in jnp.split(mod[:bsz_p], 6, axis=-1)]
    mod_s = [jnp.repeat(m, seq_s, axis=0) for m in jnp.split(mod[bsz_p:], 6, axis=-1)]

    zeros_state = jnp.zeros((bsz_p, SSM_LANES), F32)
    yp, ktp, vp, rp, ip = _group_forward(
        x_prompt, mod_p, jnp.arange(seq_p), zeros_state, zeros_state, None, lam, wts,
        per_token_mod=False, ts=min(256, seq_p), seq_major_tokens=None)

    cache_kt = jnp.transpose(cache_k[l], (0, 2, 3, 1)).reshape(n_pool, ATTN_W, page)
    cache_v2 = cache_v[l].reshape(n_pool, page * N_HEADS, HEAD_W)
    n_s = bsz_s * seq_s
    pos_s = past_len + jnp.tile(jnp.arange(seq_s), bsz_s)
    ys, kts, vs, rs, is_ = _group_forward(
        x_sample.reshape(1, n_s, D_MODEL), mod_s, pos_s,
        state_ssm_re[l].reshape(bsz_s, SSM_LANES), state_ssm_im[l].reshape(bsz_s, SSM_LANES),
        (page_table, cache_kt, cache_v2), lam, wts,
        per_token_mod=True, ts=n_s, seq_major_tokens=(bsz_s, seq_s))

    k_prompt = jnp.transpose(ktp.reshape(bsz_p, 2 * N_HEADS, HEAD_DIM, seq_p), (0, 3, 1, 2))[None]
    v_prompt = vp.reshape(bsz_p, seq_p, N_HEADS, HEAD_W)[None]
    k_sample = jnp.transpose(kts.reshape(2 * N_HEADS, HEAD_DIM, bsz_s, seq_s), (2, 3, 0, 1))[None]
    v_sample = vs.reshape(bsz_s, seq_s, N_HEADS, HEAD_W)[None]
    state = lambda h, n: h.reshape(n, SSM_GROUPS, SSM_STATE)[None]
    return (yp, ys.reshape(bsz_s, seq_s, D_MODEL), k_prompt, v_prompt, state(rp, bsz_p), state(ip, bsz_p),
            k_sample, v_sample, state(rs, bsz_s), state(is_, bsz_s))
```

```python
import functools
import math

import jax
import jax.numpy as jnp
from jax import lax
from jax.experimental import pallas as pl
from jax.experimental.pallas import tpu as pltpu

D_MODEL = 1024
SSM_WIDTH = D_MODEL // 2
SSM_GROUP = 16
SSM_GROUPS = SSM_WIDTH // SSM_GROUP
SSM_STATE = 64
SSM_LANES = SSM_GROUPS * SSM_STATE
SSM_TILE_IN = 256
SSM_TILE_STATE = SSM_TILE_IN // SSM_GROUP * SSM_STATE
MIN_DECAY = 1e-4
N_HEADS = 4
HEAD_DIM = 64
HEAD_W = 2 * HEAD_DIM
ATTN_W = N_HEADS * HEAD_W
ROT_DIM = HEAD_DIM // 4
ROT_HALF = ROT_DIM // 2
ROPE_THETA = 500000.0
NEG_INF = -1e30
N_EXPERTS = 32
TOP_K = 4
D_FF = D_MODEL
SWIGLU_LIMIT = 7.0
SWIGLU_ALPHA = 1.702
EPS = 1e-6
LAM_INIT = 0.8 - 0.6 * math.exp(-0.3 * 0)
LOG2_E = 1.4426950408889634

VMEM_LIMIT_BYTES = 56 * 1024 * 1024
LANE = 128
SUBLANES = 8
ROW_CHUNKS = D_MODEL // LANE
assert ROW_CHUNKS == SUBLANES

BF16 = jnp.bfloat16
F32 = jnp.float32


def _cparams(sem):
    return pltpu.CompilerParams(dimension_semantics=sem, vmem_limit_bytes=VMEM_LIMIT_BYTES)


def _dot(a, b):
    return jnp.dot(a, b, preferred_element_type=F32)


def _rms(x, g):
    return x * lax.rsqrt(jnp.mean(x * x, axis=-1, keepdims=True) + EPS) * g


def _mod_kernel(c_ref, w_ref, b_ref, o_ref):
    c = c_ref[...]
    s = (c * jax.nn.sigmoid(c)).astype(BF16)
    o_ref[...] = _dot(s, w_ref[...].astype(BF16)) + b_ref[...]


def _modulation(c, w_ada, b_ada):
    n = c.shape[0]
    return pl.pallas_call(
        _mod_kernel,
        grid=(6,),
        in_specs=[pl.BlockSpec((n, D_MODEL), lambda j: (0, 0)),
                  pl.BlockSpec((D_MODEL, D_MODEL), lambda j: (0, j)),
                  pl.BlockSpec((1, D_MODEL), lambda j: (0, j))],
        out_specs=pl.BlockSpec((n, D_MODEL), lambda j: (0, j)),
        out_shape=jax.ShapeDtypeStruct((n, 6 * D_MODEL), F32),
        compiler_params=_cparams(("arbitrary",)),
        name="adaln_mod",
    )(c, w_ada, b_ada.reshape(1, 6 * D_MODEL))


def _inproj_kernel(x_ref, shift_ref, scale_ref, g_ref, wu_ref, wq_ref, wkt_ref, wv_ref, wg_ref,
                   cosq_ref, sina_ref, sinb_ref, cost_ref, sint_ref,
                   u_ref, q_ref, kt_ref, v_ref, gate_ref):
    x = x_ref[...]
    h = _rms(x, g_ref[...]) * (1.0 + scale_ref[...]) + shift_ref[...]
    hb = h.astype(BF16)
    u_ref[...] = _dot(hb, wu_ref[...])
    v = _dot(hb, wv_ref[...])
    for hd in range(N_HEADS):
        v_ref[pl.ds(hd, x.shape[0], stride=N_HEADS), :] = v[:, hd * HEAD_W:(hd + 1) * HEAD_W]
    gate_ref[...] = jax.nn.sigmoid(_dot(hb, wg_ref[...])).astype(BF16)
    q = _dot(hb, wq_ref[...])
    blocks = [q[:, j:j + LANE] for j in range(0, ATTN_W, LANE)]
    q_up = jnp.concatenate([pltpu.roll(b, LANE - ROT_HALF, 1) for b in blocks], axis=1)
    q_dn = jnp.concatenate([pltpu.roll(b, ROT_HALF, 1) for b in blocks], axis=1)
    q = q * cosq_ref[...] + q_up * sina_ref[...] + q_dn * sinb_ref[...]
    q_ref[...] = (q * (HEAD_DIM ** -0.5 * LOG2_E)).astype(BF16)
    kt = lax.dot_general(wkt_ref[...], hb, (((1,), (1,)), ((), ())), preferred_element_type=F32)
    cos_t = cost_ref[...]
    sin_t = sint_ref[...]
    for hh in range(2 * N_HEADS):
        r0 = hh * HEAD_DIM
        x1 = kt[r0:r0 + ROT_HALF, :]
        x2 = kt[r0 + ROT_HALF:r0 + ROT_DIM, :]
        kt_ref[r0:r0 + ROT_HALF, :] = x1 * cos_t - x2 * sin_t
        kt_ref[r0 + ROT_HALF:r0 + ROT_DIM, :] = x2 * cos_t + x1 * sin_t
        kt_ref[r0 + ROT_DIM:r0 + HEAD_DIM, :] = kt[r0 + ROT_DIM:r0 + HEAD_DIM, :]


def _rope_tables(pos):
    inv_freq = ROPE_THETA ** (-jnp.arange(ROT_HALF, dtype=F32) * (2.0 / ROT_DIM))
    ang = pos.astype(F32)[:, None] * inv_freq[None, :]
    cos, sin = jnp.cos(ang), jnp.sin(ang)
    s = pos.shape[0]
    pad = HEAD_DIM - ROT_DIM
    cos_h = jnp.concatenate([cos, cos, jnp.ones((s, pad), F32)], axis=1)
    sina_h = jnp.concatenate([-sin, jnp.zeros((s, ROT_HALF + pad), F32)], axis=1)
    sinb_h = jnp.concatenate([jnp.zeros((s, ROT_HALF), F32), sin, jnp.zeros((s, pad), F32)], axis=1)
    reps = ATTN_W // HEAD_DIM
    return (jnp.tile(cos_h, (1, reps)), jnp.tile(sina_h, (1, reps)), jnp.tile(sinb_h, (1, reps)),
            cos.T, sin.T)


def _in_projection(x, shift, scale, g, weights, tables, ts, per_token_mod):
    nb, s, _ = x.shape
    ns = s // ts
    wu, wq, wkt, wv, wg = weights
    cosq, sina, sinb, cost, sint = tables
    if per_token_mod:
        mod_spec = pl.BlockSpec((ts, D_MODEL), lambda b, i: (b * ns + i, 0))
    else:
        mod_spec = pl.BlockSpec((None, 1, D_MODEL), lambda b, i: (b, 0, 0))
    const = lambda shape: pl.BlockSpec(shape, lambda b, i: (0,) * len(shape))
    tab = pl.BlockSpec((ts, ATTN_W), lambda b, i: (i, 0))
    tab_t = pl.BlockSpec((ROT_HALF, ts), lambda b, i: (0, i))
    return pl.pallas_call(
        _inproj_kernel,
        grid=(nb, ns),
        in_specs=[pl.BlockSpec((None, ts, D_MODEL), lambda b, i: (b, i, 0)), mod_spec, mod_spec,
                  const((1, D_MODEL)),
                  const((D_MODEL, SSM_WIDTH)), const((D_MODEL, ATTN_W)), const((ATTN_W, D_MODEL)),
                  const((D_MODEL, ATTN_W)), const((D_MODEL, 2 * D_MODEL)),
                  tab, tab, tab, tab_t, tab_t],
        out_specs=[pl.BlockSpec((ts, SSM_WIDTH), lambda b, i: (i, b)),
                   pl.BlockSpec((None, ts, ATTN_W), lambda b, i: (b, i, 0)),
                   pl.BlockSpec((None, ATTN_W, ts), lambda b, i: (b, 0, i)),
                   pl.BlockSpec((None, ts * N_HEADS, HEAD_W), lambda b, i: (b, i, 0)),
                   pl.BlockSpec((None, ts, 2 * D_MODEL), lambda b, i: (b, i, 0))],
        out_shape=[jax.ShapeDtypeStruct((s, nb * SSM_WIDTH), F32),
                   jax.ShapeDtypeStruct((nb, s, ATTN_W), BF16),
                   jax.ShapeDtypeStruct((nb, ATTN_W, s), F32),
                   jax.ShapeDtypeStruct((nb, s * N_HEADS, HEAD_W), F32),
                   jax.ShapeDtypeStruct((nb, s, 2 * D_MODEL), BF16)],
        compiler_params=_cparams(("arbitrary", "arbitrary")),
        name="in_proj",
    )(x, shift, scale, g, wu, wq, wkt, wv, wg, cosq, sina, sinb, cost, sint)


def _softmax_step(s, v_bf, m_prev, l_prev, acc_prev):
    m_next = jnp.maximum(m_prev, jnp.max(s, axis=1, keepdims=True))
    p = jnp.exp2(s - jnp.tile(m_next, (1, s.shape[1] // LANE)))
    alpha = jnp.exp2(m_prev - m_next)
    l_next = alpha * l_prev + jnp.sum(p, axis=1, keepdims=True)
    acc_next = alpha * acc_prev + _dot(p.astype(BF16), v_bf)
    return m_next, l_next, acc_next


def _diff_head_out(lam, acc0, l0, acc1, l1, g):
    o = acc0 / l0 - lam * (acc1 / l1)
    return _rms(o, g) * (1.0 - LAM_INIT)


def _init_softmax_state(m_ref, l_ref, acc_ref):
    m_ref[...] = jnp.full(m_ref.shape, NEG_INF, F32)
    l_ref[...] = jnp.zeros(l_ref.shape, F32)
    acc_ref[...] = jnp.zeros(acc_ref.shape, F32)


def _prompt_attn_kernel(qi_ref, ki_ref, lam_ref, q_ref, kt_ref, v_ref, g_ref, o_ref, m_ref, l_ref, acc_ref, *, t):
    pair = pl.program_id(1)
    qi = qi_ref[pair]
    ki = ki_ref[pair]

    @pl.when(ki == 0)
    def _():
        _init_softmax_state(m_ref, l_ref, acc_ref)

    def update(masked):
        if masked:
            visible = (lax.broadcasted_iota(jnp.int32, (t, t), 1) <= lax.broadcasted_iota(jnp.int32, (t, t), 0))
        for h in range(N_HEADS):
            v_bf = v_ref[pl.ds(h, t, stride=N_HEADS), :].astype(BF16)
            for c in range(2):
                hc = 2 * h + c
                r0 = hc * HEAD_DIM
                s = _dot(q_ref[:, r0:r0 + HEAD_DIM], kt_ref[r0:r0 + HEAD_DIM, :].astype(BF16))
                if masked:
                    s = jnp.where(visible, s, NEG_INF)
                m_ref[hc], l_ref[hc], acc_ref[hc] = _softmax_step(s, v_bf, m_ref[hc], l_ref[hc], acc_ref[hc])

    @pl.when(ki < qi)
    def _():
        update(False)

    @pl.when(ki == qi)
    def _():
        update(True)
        lam = lam_ref[0]
        g = g_ref[...]
        for h in range(N_HEADS):
            o_ref[:, h * HEAD_W:(h + 1) * HEAD_W] = _diff_head_out(
                lam, acc_ref[2 * h], l_ref[2 * h], acc_ref[2 * h + 1], l_ref[2 * h + 1], g)


def _prompt_attention(lam, q, kt, v, subln_g, t):
    b, s, _ = q.shape
    n = s // t
    pairs = [(qi, ki) for qi in range(n) for ki in range(qi + 1)]
    qi_tab = jnp.asarray([p[0] for p in pairs], jnp.int32)
    ki_tab = jnp.asarray([p[1] for p in pairs], jnp.int32)
    kern = functools.partial(_prompt_attn_kernel, t=t)
    grid_spec = pltpu.PrefetchScalarGridSpec(
        num_scalar_prefetch=3,
        grid=(b, len(pairs)),
        in_specs=[pl.BlockSpec((None, t, ATTN_W), lambda bi, p, qt, kt_, lam_: (bi, qt[p], 0)),
                  pl.BlockSpec((None, ATTN_W, t), lambda bi, p, qt, kt_, lam_: (bi, 0, kt_[p])),
                  pl.BlockSpec((None, t * N_HEADS, HEAD_W), lambda bi, p, qt, kt_, lam_: (bi, kt_[p], 0)),
                  pl.BlockSpec((1, HEAD_W), lambda bi, p, qt, kt_, lam_: (0, 0))],
        out_specs=pl.BlockSpec((None, t, ATTN_W), lambda bi, p, qt, kt_, lam_: (bi, qt[p], 0)),
        scratch_shapes=[pltpu.VMEM((2 * N_HEADS, t, LANE), F32), pltpu.VMEM((2 * N_HEADS, t, LANE), F32),
                        pltpu.VMEM((2 * N_HEADS, t, HEAD_W), F32)])
    return pl.pallas_call(
        kern, grid_spec=grid_spec,
        out_shape=jax.ShapeDtypeStruct((b, s, ATTN_W), F32),
        compiler_params=_cparams(("arbitrary", "arbitrary")),
        name="prompt_attn",
    )(qi_tab, ki_tab, lam, q, kt, v, subln_g)


def _sample_attn_kernel(pt_ref, lam_ref, q_ref, ktn_ref, vn_ref, g_ref, *rest, pages, page):
    k_refs = rest[:pages]
    v_refs = rest[pages:2 * pages]
    o_ref, kcat_ref, vcat_ref, m_ref, l_ref, acc_ref = rest[2 * pages:]
    j = pl.program_id(1)
    nq = q_ref.shape[0]
    hrows = 2 * nq

    @pl.when(j == 0)
    def _():
        _init_softmax_state(m_ref, l_ref, acc_ref)

    for i in range(pages):
        kcat_ref[:, i * page:(i + 1) * page] = k_refs[i][...].astype(BF16)
        for h in range(N_HEADS):
            vcat_ref[h, i * page:(i + 1) * page, :] = v_refs[i][pl.ds(h, page, stride=N_HEADS), :].astype(BF16)

    q = q_ref[...]
    q_parts = [q[:, hc * HEAD_DIM:(hc + 1) * HEAD_DIM] for hc in range(2 * N_HEADS)]
    s = jnp.concatenate([_dot(q_parts[hc], kcat_ref[hc * HEAD_DIM:(hc + 1) * HEAD_DIM, :])
                         for hc in range(2 * N_HEADS)], axis=0)
    m_prev = m_ref[...]
    m_next = jnp.maximum(m_prev, jnp.max(s, axis=1, keepdims=True))
    p = jnp.exp2(s - jnp.tile(m_next, (1, s.shape[1] // LANE)))
    alpha = jnp.exp2(m_prev - m_next)
    l_ref[...] = alpha * l_ref[...] + jnp.sum(p, axis=1, keepdims=True)
    pb = p.astype(BF16)
    pv = jnp.concatenate([_dot(pb[h * hrows:(h + 1) * hrows], vcat_ref[h]) for h in range(N_HEADS)], axis=0)
    acc_ref[...] = alpha * acc_ref[...] + pv
    m_ref[...] = m_next

    @pl.when(j == pl.num_programs(1) - 1)
    def _():
        rows = 2 * N_HEADS * nq
        s_new = jnp.concatenate([_dot(q_parts[hc], ktn_ref[hc * HEAD_DIM:(hc + 1) * HEAD_DIM, :].astype(BF16))
                                 for hc in range(2 * N_HEADS)], axis=0)
        qpos = lax.broadcasted_iota(jnp.int32, (rows, nq), 0) % nq
        kpos = lax.broadcasted_iota(jnp.int32, (rows, nq), 1)
        s_new = jnp.where(kpos <= qpos, s_new, NEG_INF)
        m_old = m_ref[...]
        m_fin = jnp.maximum(m_old, jnp.max(s_new, axis=1, keepdims=True))
        p_new = jnp.exp2(s_new - m_fin[:, :nq])
        a_fin = jnp.exp2(m_old - m_fin)
        l_fin = a_fin * l_ref[...] + jnp.sum(p_new, axis=1, keepdims=True)
        pnb = p_new.astype(BF16)
        pv_new = jnp.concatenate(
            [_dot(pnb[h * hrows:(h + 1) * hrows], vn_ref[:, h * HEAD_W:(h + 1) * HEAD_W].astype(BF16))
             for h in range(N_HEADS)], axis=0)
        acc_fin = a_fin * acc_ref[...] + pv_new
        lam = lam_ref[0]
        g = g_ref[...]
        for h in range(N_HEADS):
            r0 = h * hrows
            o_ref[:, h * HEAD_W:(h + 1) * HEAD_W] = _diff_head_out(
                lam, acc_fin[r0:r0 + nq], l_fin[r0:r0 + nq], acc_fin[r0 + nq:r0 + hrows], l_fin[r0 + nq:r0 + hrows], g)


def _sample_attention(page_table, lam, q, kt_new, v_new, subln_g, cache_kt, cache_v2, pages):
    b, nq, _ = q.shape
    n_pages = page_table.shape[1]
    page = cache_kt.shape[2]
    nj = n_pages // pages
    rows = 2 * N_HEADS * nq
    kern = functools.partial(_sample_attn_kernel, pages=pages, page=page)

    def k_spec(i):
        return pl.BlockSpec((None, ATTN_W, page), lambda bi, j, pt, lam_: (pt[bi, j * pages + i], 0, 0))

    def v_spec(i):
        return pl.BlockSpec((None, page * N_HEADS, HEAD_W), lambda bi, j, pt, lam_: (pt[bi, j * pages + i], 0, 0))

    grid_spec = pltpu.PrefetchScalarGridSpec(
        num_scalar_prefetch=2,
        grid=(b, nj),
        in_specs=[pl.BlockSpec((None, nq, ATTN_W), lambda bi, j, pt, lam_: (bi, 0, 0)),
                  pl.BlockSpec((None, ATTN_W, nq), lambda bi, j, pt, lam_: (bi, 0, 0)),
                  pl.BlockSpec((None, nq, ATTN_W), lambda bi, j, pt, lam_: (bi, 0, 0)),
                  pl.BlockSpec((1, HEAD_W), lambda bi, j, pt, lam_: (0, 0))]
                 + [k_spec(i) for i in range(pages)] + [v_spec(i) for i in range(pages)],
        out_specs=pl.BlockSpec((None, nq, ATTN_W), lambda bi, j, pt, lam_: (bi, 0, 0)),
        scratch_shapes=[pltpu.VMEM((ATTN_W, pages * page), BF16),
                        pltpu.VMEM((N_HEADS, pages * page, HEAD_W), BF16),
                        pltpu.VMEM((rows, LANE), F32), pltpu.VMEM((rows, LANE), F32),
                        pltpu.VMEM((rows, HEAD_W), F32)])
    return pl.pallas_call(
        kern, grid_spec=grid_spec,
        out_shape=jax.ShapeDtypeStruct((b, nq, ATTN_W), F32),
        compiler_params=_cparams(("arbitrary", "arbitrary")),
        name="sample_attn",
    )(page_table, lam, q, kt_new, v_new, subln_g, *([cache_kt] * pages), *([cache_v2] * pages))


def _ssm_kernel(u_ref, h0r_ref, h0i_ref, ar_ref, ai_ref, bbr_ref, bbi_ref, ccr_ref, cci_ref, d_ref,
                wglu_ref, bglu_ref, y_ref, hr_out, hi_out, sr_ref, si_ref, br_ref, bi_ref, tm_ref,
                *, steps, nb, lane_chunk, batch_on_lanes):
    i = pl.program_id(0)

    @pl.when(i == 0)
    def _():
        sr_ref[...] = h0r_ref[...]
        si_ref[...] = h0i_ref[...]

    n_slabs = SSM_WIDTH // LANE
    if batch_on_lanes:
        for b in range(nb):
            for c in range(n_slabs):
                col = b * SSM_WIDTH + c * LANE
                tm_ref[c, pl.ds(b, steps, stride=nb), :] = u_ref[:, col:col + LANE]
        u = jnp.concatenate([tm_ref[c] for c in range(n_slabs)], axis=1)
    else:
        u = u_ref[...]
    ub = u.astype(BF16)
    for m in range(SSM_WIDTH // SSM_TILE_IN):
        cin = slice(m * SSM_TILE_IN, (m + 1) * SSM_TILE_IN)
        cst = slice(m * SSM_TILE_STATE, (m + 1) * SSM_TILE_STATE)
        br_ref[:, cst] = _dot(ub[:, cin], bbr_ref[cin, cst])
        bi_ref[:, cst] = _dot(ub[:, cin], bbi_ref[cin, cst])

    for c0 in range(0, SSM_LANES, lane_chunk):
        lanes = pl.ds(c0, lane_chunk)
        a_r = jnp.broadcast_to(ar_ref[:, lanes], (nb, lane_chunk))
        a_i = jnp.broadcast_to(ai_ref[:, lanes], (nb, lane_chunk))

        def step(t, carry):
            h_r, h_i = carry
            rows = pl.ds(pl.multiple_of(t * nb, nb), nb)
            n_r = a_r * h_r - a_i * h_i + br_ref[rows, lanes]
            n_i = a_r * h_i + a_i * h_r + bi_ref[rows, lanes]
            br_ref[rows, lanes] = n_r
            bi_ref[rows, lanes] = n_i
            return n_r, n_i

        h_r, h_i = lax.fori_loop(0, steps, step, (sr_ref[:, lanes], si_ref[:, lanes]), unroll=4)
        sr_ref[:, lanes] = h_r
        si_ref[:, lanes] = h_i

    y_parts = []
    for m in range(SSM_WIDTH // SSM_TILE_IN):
        cin = slice(m * SSM_TILE_IN, (m + 1) * SSM_TILE_IN)
        cst = slice(m * SSM_TILE_STATE, (m + 1) * SSM_TILE_STATE)
        y_parts.append(_dot(br_ref[:, cst].astype(BF16), ccr_ref[cst, cin])
                       - _dot(bi_ref[:, cst].astype(BF16), cci_ref[cst, cin]))
    y = jnp.concatenate(y_parts, axis=1) + d_ref[...] * u
    y = jax.nn.gelu(y, approximate=True)
    y = y * jax.nn.sigmoid(_dot(y.astype(BF16), wglu_ref[...]) + bglu_ref[...])
    if batch_on_lanes:
        for c in range(n_slabs):
            tm_ref[c] = y[:, c * LANE:(c + 1) * LANE]
        for b in range(nb):
            for c in range(n_slabs):
                col = b * SSM_WIDTH + c * LANE
                y_ref[:, col:col + LANE] = tm_ref[c, pl.ds(b, steps, stride=nb), :]
    else:
        y_ref[...] = y

    @pl.when(i == pl.num_programs(0) - 1)
    def _():
        hr_out[...] = sr_ref[...]
        hi_out[...] = si_ref[...]


def _ssm_params(a_re, a_im, log_dt, b_re, b_im, c_re, c_im, d_skip):
    lam_re = jnp.minimum(a_re.astype(F32), -MIN_DECAY)
    lam_im = a_im.astype(F32)
    dt = jnp.exp(log_dt.astype(F32))[:, None]
    mag = jnp.exp(lam_re * dt)
    ab_re = mag * jnp.cos(lam_im * dt)
    ab_im = mag * jnp.sin(lam_im * dt)
    den = lam_re * lam_re + lam_im * lam_im
    n_re = ab_re - 1.0
    n_im = ab_im
    zoh_re = (n_re * lam_re + n_im * lam_im) / den
    zoh_im = (n_im * lam_re - n_re * lam_im) / den
    br, bi = b_re.astype(F32), b_im.astype(F32)
    bb_re = zoh_re[..., None] * br - zoh_im[..., None] * bi
    bb_im = zoh_re[..., None] * bi + zoh_im[..., None] * br
    eye = jnp.eye(SSM_GROUPS, dtype=F32)

    def in_blockdiag(bb):
        m = jnp.einsum('gpc,gh->gchp', bb, eye)
        return m.reshape(SSM_WIDTH, SSM_LANES).astype(BF16)

    def out_blockdiag(cc):
        m = jnp.einsum('gcp,gh->gphc', cc.astype(F32), eye)
        return m.reshape(SSM_LANES, SSM_WIDTH).astype(BF16)

    return (ab_re.reshape(1, SSM_LANES), ab_im.reshape(1, SSM_LANES), in_blockdiag(bb_re), in_blockdiag(bb_im),
            out_blockdiag(c_re), out_blockdiag(c_im), d_skip.astype(F32).reshape(1, SSM_WIDTH))


def _ssm_scan(u, h0_re, h0_im, params, w_glu, b_glu, nb, steps, batch_on_lanes):
    rows = steps * nb
    n_chunks = u.shape[0] * u.shape[1] // (rows * SSM_WIDTH)
    ab_re, ab_im, bbr, bbi, ccr, cci, d_row = params
    lane_chunk = min(SSM_LANES, 8 * 1024 // nb)
    kern = functools.partial(_ssm_kernel, steps=steps, nb=nb, lane_chunk=lane_chunk, batch_on_lanes=batch_on_lanes)
    const = lambda shape: pl.BlockSpec(shape, lambda i: (0,) * len(shape))
    io_block = (steps, nb * SSM_WIDTH) if batch_on_lanes else (rows, SSM_WIDTH)
    return pl.pallas_call(
        kern, grid=(n_chunks,),
        in_specs=[pl.BlockSpec(io_block, lambda i: (i, 0)),
                  const((nb, SSM_LANES)), const((nb, SSM_LANES)),
                  const((1, SSM_LANES)), const((1, SSM_LANES)),
                  const((SSM_WIDTH, SSM_LANES)), const((SSM_WIDTH, SSM_LANES)),
                  const((SSM_LANES, SSM_WIDTH)), const((SSM_LANES, SSM_WIDTH)),
                  const((1, SSM_WIDTH)), const((SSM_WIDTH, SSM_WIDTH)), const((1, SSM_WIDTH))],
        out_specs=[pl.BlockSpec(io_block, lambda i: (i, 0)),
                   const((nb, SSM_LANES)), const((nb, SSM_LANES))],
        out_shape=[jax.ShapeDtypeStruct(u.shape, F32),
                   jax.ShapeDtypeStruct((nb, SSM_LANES), F32), jax.ShapeDtypeStruct((nb, SSM_LANES), F32)],
        scratch_shapes=[pltpu.VMEM((nb, SSM_LANES), F32), pltpu.VMEM((nb, SSM_LANES), F32),
                        pltpu.VMEM((rows, SSM_LANES), F32), pltpu.VMEM((rows, SSM_LANES), F32),
                        pltpu.VMEM((SSM_WIDTH // LANE, rows, LANE), F32)],
        compiler_params=_cparams(("arbitrary",)),
        name="ssm_scan",
    )(u, h0_re, h0_im, ab_re, ab_im, bbr, bbi, ccr, cci, d_row, w_glu, b_glu)


def _post_kernel(yssm_ref, yattn_ref, gate_ref, x_ref, gate1_ref, shift2_ref, scale2_ref,
                 pssm_ref, pattn_ref, wo_ref, g2_ref, wrh_ref, wrl_ref, br_ref,
                 x1_ref, h2_ref, ridx_ref, rgate_ref, cnt_ref, carry_ref, *, ts):
    @pl.when(jnp.logical_and(pl.program_id(0) == 0, pl.program_id(1) == 0))
    def _():
        carry_ref[...] = jnp.zeros(carry_ref.shape, F32)

    gates = gate_ref[...]
    a = _dot(yssm_ref[...].astype(BF16), pssm_ref[...])
    b = _dot(yattn_ref[...].astype(BF16), pattn_ref[...])
    merged = gates[:, :D_MODEL] * a + gates[:, D_MODEL:] * b
    x1 = x_ref[...] + gate1_ref[...] * _dot(merged.astype(BF16), wo_ref[...])
    x1_ref[...] = x1
    h2 = _rms(x1, g2_ref[...]) * (1.0 + scale2_ref[...]) + shift2_ref[...]
    _store_tile_rows(h2_ref, h2)

    h_hi = h2.astype(BF16)
    h_lo = (h2 - h_hi.astype(F32)).astype(BF16)
    nt = (((1,), (1,)), ((), ()))
    w_hi = wrh_ref[...]
    logits = (lax.dot_general(w_hi, h_hi, nt, preferred_element_type=F32)
              + lax.dot_general(w_hi, h_lo, nt, preferred_element_type=F32)
              + lax.dot_general(wrl_ref[...], h_hi, nt, preferred_element_type=F32)) + br_ref[...]

    expert = lax.broadcasted_iota(jnp.int32, (N_EXPERTS, ts), 0)
    work = logits
    idxs, vals = [], []
    for _ in range(TOP_K):
        mk = jnp.max(work, axis=0, keepdims=True)
        ik = jnp.min(jnp.where(work == mk, expert, N_EXPERTS), axis=0, keepdims=True)
        idxs.append(ik)
        vals.append(mk)
        work = jnp.where(expert == ik, -jnp.inf, work)
    exps = [jnp.exp(v - vals[0]) for v in vals]
    den = exps[0] + exps[1] + exps[2] + exps[3]
    rgate_ref[...] = jnp.concatenate([e / den for e in exps], axis=0)

    chosen = jnp.zeros((N_EXPERTS, ts), F32)
    for ik in idxs:
        chosen = chosen + jnp.where(expert == ik, 1.0, 0.0)
    earlier = jnp.where(lax.broadcasted_iota(jnp.int32, (ts, ts), 0) < lax.broadcasted_iota(jnp.int32, (ts, ts), 1),
                        1.0, 0.0).astype(BF16)
    before = _dot(chosen.astype(BF16), earlier) + carry_ref[...]
    ranks = [jnp.sum(jnp.where(expert == ik, before, 0.0), axis=0, keepdims=True).astype(jnp.int32) for ik in idxs]
    ridx_ref[...] = jnp.concatenate(idxs + ranks, axis=0)
    carry_ref[...] = carry_ref[...] + jnp.sum(chosen, axis=1, keepdims=True)
    cnt_ref[...] = carry_ref[...]


def _post_mixer(yssm_tm, yattn, gate, x, gate1, shift2, scale2, weights, ts, per_token_mod):
    nb, s, _ = x.shape
    ns = s // ts
    p_ssm, p_attn, w_o, g2, wr_hi, wr_lo, b_r = weights
    if per_token_mod:
        mod_spec = pl.BlockSpec((ts, D_MODEL), lambda b, i: (b * ns + i, 0))
    else:
        mod_spec = pl.BlockSpec((None, 1, D_MODEL), lambda b, i: (b, 0, 0))
    const = lambda shape: pl.BlockSpec(shape, lambda b, i: (0,) * len(shape))
    tok = lambda w: pl.BlockSpec((None, ts, w), lambda b, i: (b, i, 0))
    tok_t = lambda r: pl.BlockSpec((None, r, ts), lambda b, i: (b, 0, i))
    return pl.pallas_call(
        functools.partial(_post_kernel, ts=ts), grid=(nb, ns),
        in_specs=[pl.BlockSpec((ts, SSM_WIDTH), lambda b, i: (i, b)), tok(ATTN_W), tok(2 * D_MODEL), tok(D_MODEL),
                  mod_spec, mod_spec, mod_spec,
                  const((SSM_WIDTH, D_MODEL)), const((ATTN_W, D_MODEL)), const((D_MODEL, D_MODEL)),
                  const((1, D_MODEL)), const((N_EXPERTS, D_MODEL)), const((N_EXPERTS, D_MODEL)),
                  const((N_EXPERTS, 1))],
        out_specs=[tok(D_MODEL), pl.BlockSpec((None, ts * ROW_CHUNKS, LANE), lambda b, i: (b, i, 0)),
                   tok_t(2 * TOP_K), tok_t(TOP_K), const((N_EXPERTS, 1))],
        out_shape=[jax.ShapeDtypeStruct((nb, s, D_MODEL), F32),
                   jax.ShapeDtypeStruct((nb, s * ROW_CHUNKS, LANE), F32),
                   jax.ShapeDtypeStruct((nb, 2 * TOP_K, s), jnp.int32),
                   jax.ShapeDtypeStruct((nb, TOP_K, s), F32),
                   jax.ShapeDtypeStruct((N_EXPERTS, 1), F32)],
        scratch_shapes=[pltpu.VMEM((N_EXPERTS, 1), F32)],
        compiler_params=_cparams(("arbitrary", "arbitrary")),
        name="post_mixer",
    )(yssm_tm, yattn, gate, x, gate1, shift2, scale2, p_ssm, p_attn, w_o, g2, wr_hi, wr_lo, b_r)


def _store_tile_rows(ref, x, row0=0):
    n = x.shape[0]
    for j in range(ROW_CHUNKS):
        ref[pl.ds(row0 * ROW_CHUNKS + j, n, stride=ROW_CHUNKS), :] = x[:, j * LANE:(j + 1) * LANE]


def _load_tile_rows(ref, n, row0=0):
    return jnp.concatenate([ref[pl.ds(row0 * ROW_CHUNKS + j, n, stride=ROW_CHUNKS), :]
                            for j in range(ROW_CHUNKS)], axis=1)


def _row_copy(src_hbm, src_start, dst_ref, dst_row, sem):
    return pltpu.make_async_copy(src_hbm.at[pl.ds(src_start, ROW_CHUNKS)],
                                 dst_ref.at[pl.ds(dst_row * ROW_CHUNKS, ROW_CHUNKS)], sem)


def _start_rows(src_hbm, idx_ref, dst_ref, sem, n):
    for r in range(n):
        src_start = pl.multiple_of(idx_ref[0, r] * ROW_CHUNKS, ROW_CHUNKS)
        _row_copy(src_hbm, src_start, dst_ref, r, sem).start(priority=r % 2)


def _wait_rows(src_hbm, dst_ref, sem, n):
    for r in range(n):
        _row_copy(src_hbm, 0, dst_ref, r, sem).wait()


def _moe_kernel(iblk_ref, iexp_ref, ilo_ref, ihi_ref, tok_ref, tok_next_ref, h_hbm, wup_ref, bup_ref,
                wdn_ref, bdn_ref, o_ref, xbuf0_ref, xbuf1_ref, merged_ref, wupb_ref, wdnb_ref, sem_ref, *, blk):
    w = pl.program_id(0)
    last = pl.num_programs(0) - 1

    @pl.when(w == 0)
    def _():
        _start_rows(h_hbm, tok_ref, xbuf0_ref, sem_ref.at[0], blk)
        merged_ref[...] = jnp.zeros(merged_ref.shape, F32)

    @pl.when(jnp.logical_or(w == 0, iexp_ref[w] != iexp_ref[jnp.maximum(w - 1, 0)]))
    def _():
        wupb_ref[...] = wup_ref[...].astype(BF16)
        wdnb_ref[...] = wdn_ref[...].astype(BF16)

    def step(cur_ref, cur_sem, nxt_ref, nxt_sem):
        @pl.when(w <= last)
        def _():
            _start_rows(h_hbm, tok_next_ref, nxt_ref, nxt_sem, blk)

        _wait_rows(h_hbm, cur_ref, cur_sem, blk)
        xb = _load_tile_rows(cur_ref, blk).astype(BF16)
        gu = _dot(xb, wupb_ref[...]) + bup_ref[...]
        g_lin = jnp.minimum(gu[:, :D_FF], SWIGLU_LIMIT)
        u_lin = jnp.clip(gu[:, D_FF:], -SWIGLU_LIMIT, SWIGLU_LIMIT)
        act = g_lin * jax.nn.sigmoid(SWIGLU_ALPHA * g_lin) * (u_lin + 1.0)
        res = _dot(act.astype(BF16), wdnb_ref[...]) + bdn_ref[...]

        rows = iblk_ref[w] * blk + lax.broadcasted_iota(jnp.int32, (blk, 1), 0)
        mine = jnp.logical_and(rows >= ilo_ref[w], rows < ihi_ref[w])
        merged = jnp.where(mine, res, merged_ref[...])
        merged_ref[...] = merged
        _store_tile_rows(o_ref, merged)

        @pl.when(w == last)
        def _():
            _wait_rows(h_hbm, nxt_ref, nxt_sem, blk)

    @pl.when(w % 2 == 0)
    def _():
        step(xbuf0_ref, sem_ref.at[0], xbuf1_ref, sem_ref.at[1])

    @pl.when(w % 2 == 1)
    def _():
        step(xbuf1_ref, sem_ref.at[1], xbuf0_ref, sem_ref.at[0])


def _routed_experts(h2, row_tok, plan, w_up, b_up, w_down, b_down, blk):
    item_blk, item_exp, item_lo, item_hi = plan
    n_rows = row_tok.shape[0]
    n_blocks = n_rows // blk
    n_items = item_blk.shape[0]
    tok3 = row_tok.reshape(n_blocks, 1, blk)
    kern = functools.partial(_moe_kernel, blk=blk)
    last = n_items - 1
    grid_spec = pltpu.PrefetchScalarGridSpec(
        num_scalar_prefetch=4,
        grid=(n_items,),
        in_specs=[pl.BlockSpec((None, 1, blk), lambda w, ib, ie, lo, hi: (ib[w], 0, 0), memory_space=pltpu.SMEM),
                  pl.BlockSpec((None, 1, blk), lambda w, ib, ie, lo, hi: (ib[jnp.minimum(w + 1, last)], 0, 0),
                               memory_space=pltpu.SMEM),
                  pl.BlockSpec(memory_space=pl.ANY),
                  pl.BlockSpec((None, D_MODEL, 2 * D_FF), lambda w, ib, ie, lo, hi: (ie[w], 0, 0)),
                  pl.BlockSpec((None, 1, 2 * D_FF), lambda w, ib, ie, lo, hi: (ie[w], 0, 0)),
                  pl.BlockSpec((None, D_FF, D_MODEL), lambda w, ib, ie, lo, hi: (ie[w], 0, 0)),
                  pl.BlockSpec((None, 1, D_MODEL), lambda w, ib, ie, lo, hi: (ie[w], 0, 0))],
        out_specs=pl.BlockSpec((blk * ROW_CHUNKS, LANE), lambda w, ib, ie, lo, hi: (ib[w], 0)),
        scratch_shapes=[pltpu.VMEM((blk * ROW_CHUNKS, LANE), F32), pltpu.VMEM((blk * ROW_CHUNKS, LANE), F32),
                        pltpu.VMEM((blk, D_MODEL), F32),
                        pltpu.VMEM((D_MODEL, 2 * D_FF), BF16), pltpu.VMEM((D_FF, D_MODEL), BF16),
                        pltpu.SemaphoreType.DMA((2,))])
    return pl.pallas_call(
        kern, grid_spec=grid_spec,
        out_shape=jax.ShapeDtypeStruct((n_rows * ROW_CHUNKS, LANE), F32),
        compiler_params=_cparams(("arbitrary",)),
        name="routed_experts",
    )(item_blk, item_exp, item_lo, item_hi, tok3, tok3, h2, w_up,
      b_up.reshape(N_EXPERTS, 1, 2 * D_FF), w_down, b_down.reshape(N_EXPERTS, 1, D_MODEL))


def _route_plan(ridx, counts, blk):
    n_tok = ridx.shape[1]
    top_i, rank = ridx[:TOP_K], ridx[TOP_K:]
    counts = counts.reshape(N_EXPERTS).astype(jnp.int32)
    ends = jnp.cumsum(counts)
    starts = ends - counts
    experts = jnp.arange(N_EXPERTS, dtype=jnp.int32)
    start_of = jnp.sum(jnp.where(top_i[..., None] == experts, starts, 0), axis=-1)
    pos = (start_of + rank).astype(jnp.int32)
    t_flat = jnp.tile(jnp.arange(n_tok, dtype=jnp.int32), TOP_K)
    _, row_tok = lax.sort_key_val(pos.reshape(-1), t_flat)
    n_blocks = n_tok * TOP_K // blk
    n_items = n_blocks + N_EXPERTS - 1
    first_blk = starts // blk
    n_blk = jnp.where(counts > 0, (ends - 1) // blk - first_blk + 1, 0)
    item_end = jnp.cumsum(n_blk)
    item_start = item_end - n_blk
    w = jnp.arange(n_items, dtype=jnp.int32)
    used = w < item_end[-1]
    e_w = jnp.minimum(jnp.sum((item_end[None, :] <= w[:, None]).astype(jnp.int32), axis=1), N_EXPERTS - 1)
    e_last = jnp.max(jnp.where(counts > 0, experts, 0))
    sel = e_w[:, None] == experts[None, :]
    of_item = lambda per_expert: jnp.sum(jnp.where(sel, per_expert[None, :], 0), axis=1)
    item_blk = jnp.where(used, of_item(first_blk) + w - of_item(item_start), n_blocks - 1).astype(jnp.int32)
    item_exp = jnp.where(used, e_w, e_last).astype(jnp.int32)
    item_lo = jnp.where(used, of_item(starts), 0).astype(jnp.int32)
    item_hi = jnp.where(used, of_item(ends), 0).astype(jnp.int32)
    return pos, row_tok, (item_blk, item_exp, item_lo, item_hi)


def _final_kernel(pos_ref, pos_next_ref, x1_ref, gate2_ref, rgate_ref, g_ref, rows_hbm, o_ref,
                  gbuf0_ref, gbuf1_ref, sem_ref, *, ts):
    i = pl.program_id(0)
    last = pl.num_programs(0) - 1
    n_rows = ts * TOP_K

    @pl.when(i == 0)
    def _():
        _start_rows(rows_hbm, pos_ref, gbuf0_ref, sem_ref.at[0], n_rows)

    def step(cur_ref, cur_sem, nxt_ref, nxt_sem):
        _wait_rows(rows_hbm, cur_ref, cur_sem, n_rows)
        _start_rows(rows_hbm, pos_next_ref, nxt_ref, nxt_sem, n_rows)
        rg = rgate_ref[...]
        ffn = rg[:, 0:1] * _load_tile_rows(cur_ref, ts)
        for k in range(1, TOP_K):
            ffn = ffn + rg[:, k:k + 1] * _load_tile_rows(cur_ref, ts, row0=k * ts)
        x = x1_ref[...] + gate2_ref[...] * ffn
        o_ref[...] = _rms(x, g_ref[...])

        @pl.when(i == last)
        def _():
            _wait_rows(rows_hbm, nxt_ref, nxt_sem, n_rows)

    @pl.when(i % 2 == 0)
    def _():
        step(gbuf0_ref, sem_ref.at[0], gbuf1_ref, sem_ref.at[1])

    @pl.when(i % 2 == 1)
    def _():
        step(gbuf1_ref, sem_ref.at[1], gbuf0_ref, sem_ref.at[0])


def _combine_final(x1, gate2, rgate, final_g, rows, pos, ts, per_token_mod):
    nb, s, _ = x1.shape
    ns = s // ts
    n_tiles = nb * ns
    pos3 = pos.reshape(TOP_K, n_tiles, ts).transpose(1, 0, 2).reshape(n_tiles, 1, ts * TOP_K)
    if per_token_mod:
        mod_spec = pl.BlockSpec((ts, D_MODEL), lambda i: (i, 0))
    else:
        mod_spec = pl.BlockSpec((None, 1, D_MODEL), lambda i: (i // ns, 0, 0))
    kern = functools.partial(_final_kernel, ts=ts)
    last = n_tiles - 1
    tok = lambda w: pl.BlockSpec((None, ts, w), lambda i: (i // ns, i % ns, 0))
    return pl.pallas_call(
        kern, grid=(n_tiles,),
        in_specs=[pl.BlockSpec((None, 1, ts * TOP_K), lambda i: (i, 0, 0), memory_space=pltpu.SMEM),
                  pl.BlockSpec((None, 1, ts * TOP_K), lambda i: (jnp.minimum(i + 1, last), 0, 0),
                               memory_space=pltpu.SMEM),
                  tok(D_MODEL), mod_spec, tok(TOP_K),
                  pl.BlockSpec((1, D_MODEL), lambda i: (0, 0)),
                  pl.BlockSpec(memory_space=pl.ANY)],
        out_specs=tok(D_MODEL),
        out_shape=jax.ShapeDtypeStruct((nb, s, D_MODEL), F32),
        scratch_shapes=[pltpu.VMEM((ts * TOP_K * ROW_CHUNKS, LANE), F32),
                        pltpu.VMEM((ts * TOP_K * ROW_CHUNKS, LANE), F32), pltpu.SemaphoreType.DMA((2,))],
        compiler_params=_cparams(("arbitrary",)),
        name="combine_final",
    )(pos3, pos3, x1, gate2, rgate, final_g, rows)


def _group_forward(x, mod, pos, h0_re, h0_im, past, lam, wts, *, per_token_mod, ts, seq_major_tokens):
    nb, s, _ = x.shape
    shift1, scale1, gate1, shift2, scale2, gate2 = mod
    u_tm, q, kt, v, gate = _in_projection(x, shift1, scale1, wts['norm1_g'], wts['w_in_parts'],
                                          _rope_tables(pos), ts, per_token_mod)
    if past is None:
        y_attn = _prompt_attention(lam, q, kt, v, wts['subln_g'], t=min(512, s))
        ssm_nb, ssm_len = nb, s
        u_scan = u_tm
    else:
        n_seq, seq_len = seq_major_tokens
        page_table, cache_kt, cache_v2 = past
        q_s = q.reshape(n_seq, seq_len, ATTN_W)
        kt_s = kt.reshape(ATTN_W, n_seq, seq_len).transpose(1, 0, 2)
        v_s = v.reshape(n_seq, seq_len, ATTN_W)
        y_attn = _sample_attention(page_table, lam, q_s, kt_s, v_s, wts['subln_g'], cache_kt, cache_v2,
                                   pages=min(16, page_table.shape[1])).reshape(nb, s, ATTN_W)
        ssm_nb, ssm_len = n_seq, seq_len
        u_scan = u_tm.reshape(n_seq, seq_len, SSM_WIDTH).transpose(1, 0, 2).reshape(s, SSM_WIDTH)
    y_tm, hT_re, hT_im = _ssm_scan(u_scan, h0_re, h0_im, wts['ssm_params'], wts['w_glu'], wts['b_glu'],
                                   nb=ssm_nb, steps=min(ssm_len, max(8, 512 // ssm_nb)),
                                   batch_on_lanes=past is None)
    if past is None:
        y_ssm = y_tm
    else:
        y_ssm = y_tm.reshape(seq_len, n_seq, SSM_WIDTH).transpose(1, 0, 2).reshape(s, SSM_WIDTH)
    x1, h2, ridx, rgate, counts = _post_mixer(y_ssm, y_attn, gate, x, gate1, shift2, scale2, wts['post'], ts,
                                               per_token_mod)
    n_tok = nb * s
    blk = min(256, n_tok * TOP_K)
    row_pos, row_tok, plan = _route_plan(ridx.transpose(1, 0, 2).reshape(2 * TOP_K, n_tok), counts, blk)
    rgate = rgate.transpose(0, 2, 1)
    rows = _routed_experts(h2.reshape(n_tok * ROW_CHUNKS, LANE), row_tok, plan,
                           wts['w_up'], wts['b_up'], wts['w_down'], wts['b_down'], blk)
    y = _combine_final(x1, gate2, rgate, wts['final_g'], rows, row_pos, min(128, s), per_token_mod)
    return y, kt, v, hT_re, hT_im


def kernel(x_prompt, x_sample, c_prompt, c_sample, cache_k, cache_v, state_ssm_re, state_ssm_im, page_table, norm1_g, w_ada, b_ada, w_in, ssm_a_re, ssm_a_im, ssm_log_dt, ssm_b_re, ssm_b_im, ssm_c_re, ssm_c_im, ssm_d, w_glu, b_glu, lambda_q1, lambda_k1, lambda_q2, lambda_k2, subln_g, p_ssm, p_attn, w_o, norm2_g, w_router, b_router, w_up, b_up, w_down, b_down, final_g):
    bsz_p, seq_p, _ = x_prompt.shape
    bsz_s, seq_s, _ = x_sample.shape
    n_pool, page = cache_k.shape[1], cache_k.shape[2]
    past_len = page_table.shape[1] * page
    l = 0

    w = w_in[l]
    c0 = SSM_WIDTH
    w_in_parts = (w[:, :c0].astype(BF16), w[:, c0:c0 + ATTN_W].astype(BF16),
                  w[:, c0 + ATTN_W:c0 + 2 * ATTN_W].T.astype(BF16),
                  w[:, c0 + 2 * ATTN_W:c0 + 3 * ATTN_W].astype(BF16), w[:, c0 + 3 * ATTN_W:].astype(BF16))
    wr_t = w_router[l].astype(F32).T
    wr_hi = wr_t.astype(BF16)
    wr_lo = (wr_t - wr_hi.astype(F32)).astype(BF16)
    wts = {
        'norm1_g': norm1_g[l].reshape(1, D_MODEL),
        'w_in_parts': w_in_parts,
        'subln_g': subln_g[l].reshape(1, HEAD_W),
        'ssm_params': _ssm_params(ssm_a_re[l], ssm_a_im[l], ssm_log_dt[l], ssm_b_re[l], ssm_b_im[l],
                                  ssm_c_re[l], ssm_c_im[l], ssm_d[l]),
        'w_glu': w_glu[l].astype(BF16), 'b_glu': b_glu[l].reshape(1, SSM_WIDTH),
        'post': (p_ssm[l].astype(BF16), p_attn[l].astype(BF16), w_o[l].astype(BF16),
                 norm2_g[l].reshape(1, D_MODEL), wr_hi, wr_lo, b_router[l].reshape(N_EXPERTS, 1)),
        'w_up': w_up[l], 'b_up': b_up[l], 'w_down': w_down[l], 'b_down': b_down[l],
        'final_g': final_g.reshape(1, D_MODEL),
    }
    lam = (jnp.exp(jnp.sum(lambda_q1[l].astype(F32) * lambda_k1[l].astype(F32)))
           - jnp.exp(jnp.sum(lambda_q2[l].astype(F32) * lambda_k2[l].astype(F32))) + LAM_INIT).reshape(1)

    mod = _modulation(jnp.concatenate([c_prompt, c_sample], axis=0), w_ada[l], b_ada[l])
    mod_p = [m.reshape(bsz_p, 1, D_MODEL) for m in jnp.split(mod[:bsz_p], 6, axis=-1)]
    mod_s = [jnp.repeat(m, seq_s, axis=0) for m in jnp.split(mod[bsz_p:], 6, axis=-1)]

    zeros_state = jnp.zeros((bsz_p, SSM_LANES), F32)
    yp, ktp, vp, rp, ip = _group_forward(
        x_prompt, mod_p, jnp.arange(seq_p), zeros_state, zeros_state, None, lam, wts,
        per_token_mod=False, ts=min(256, seq_p), seq_major_tokens=None)

    cache_kt = jnp.transpose(cache_k[l], (0, 2, 3, 1)).reshape(n_pool, ATTN_W, page)
    cache_v2 = cache_v[l].reshape(n_pool, page * N_HEADS, HEAD_W)
    n_s = bsz_s * seq_s
    pos_s = past_len + jnp.tile(jnp.arange(seq_s), bsz_s)
    ys, kts, vs, rs, is_ = _group_forward(
        x_sample.reshape(1, n_s, D_MODEL), mod_s, pos_s,
        state_ssm_re[l].reshape(bsz_s, SSM_LANES), state_ssm_im[l].reshape(bsz_s, SSM_LANES),
        (page_table, cache_kt, cache_v2), lam, wts,
        per_token_mod=True, ts=n_s, seq_major_tokens=(bsz_s, seq_s))

    k_prompt = jnp.transpose(ktp.reshape(bsz_p, 2 * N_HEADS, HEAD_DIM, seq_p), (0, 3, 1, 2))[None]
    v_prompt = vp.reshape(bsz_p, seq_p, N_HEADS, HEAD_W)[None]
    k_sample = jnp.transpose(kts.reshape(2 * N_HEADS, HEAD_DIM, bsz_s, seq_s), (2, 3, 0, 1))[None]
    v_sample = vs.reshape(bsz_s, seq_s, N_HEADS, HEAD_W)[None]
    state = lambda h, n: h.reshape(n, SSM_GROUPS, SSM_STATE)[None]
    return (yp, ys.reshape(bsz_s, seq_s, D_MODEL), k_prompt, v_prompt, state(rp, bsz_p), state(ip, bsz_p),
            k_sample, v_sample, state(rs, bsz_s), state(is_, bsz_s))
```

```python
import functools
import math

import jax
import jax.numpy as jnp
from jax import lax
from jax.experimental import pallas as pl
from jax.experimental.pallas import tpu as pltpu

D_MODEL = 1024
SSM_WIDTH = D_MODEL // 2
SSM_GROUP = 16
SSM_GROUPS = SSM_WIDTH // SSM_GROUP
SSM_STATE = 64
SSM_LANES = SSM_GROUPS * SSM_STATE
SSM_TILE_IN = 256
SSM_TILE_STATE = SSM_TILE_IN // SSM_GROUP * SSM_STATE
MIN_DECAY = 1e-4
N_HEADS = 4
HEAD_DIM = 64
HEAD_W = 2 * HEAD_DIM
ATTN_W = N_HEADS * HEAD_W
ROT_DIM = HEAD_DIM // 4
ROT_HALF = ROT_DIM // 2
ROPE_THETA = 500000.0
NEG_INF = -1e30
N_EXPERTS = 32
TOP_K = 4
D_FF = D_MODEL
SWIGLU_LIMIT = 7.0
SWIGLU_ALPHA = 1.702
EPS = 1e-6
LAM_INIT = 0.8 - 0.6 * math.exp(-0.3 * 0)
LOG2_E = 1.4426950408889634

VMEM_LIMIT_BYTES = 56 * 1024 * 1024
LANE = 128
SUBLANES = 8
ROW_CHUNKS = D_MODEL // LANE
assert ROW_CHUNKS == SUBLANES

BF16 = jnp.bfloat16
F32 = jnp.float32


def _cparams(sem):
    return pltpu.CompilerParams(dimension_semantics=sem, vmem_limit_bytes=VMEM_LIMIT_BYTES)


def _dot(a, b):
    return jnp.dot(a, b, preferred_element_type=F32)


def _rms(x, g):
    return x * lax.rsqrt(jnp.mean(x * x, axis=-1, keepdims=True) + EPS) * g


def _mod_kernel(c_ref, w_ref, b_ref, o_ref):
    c = c_ref[...]
    s = (c * jax.nn.sigmoid(c)).astype(BF16)
    o_ref[...] = _dot(s, w_ref[...].astype(BF16)) + b_ref[...]


def _modulation(c, w_ada, b_ada):
    n = c.shape[0]
    return pl.pallas_call(
        _mod_kernel,
        grid=(6,),
        in_specs=[pl.BlockSpec((n, D_MODEL), lambda j: (0, 0)),
                  pl.BlockSpec((D_MODEL, D_MODEL), lambda j: (0, j)),
                  pl.BlockSpec((1, D_MODEL), lambda j: (0, j))],
        out_specs=pl.BlockSpec((n, D_MODEL), lambda j: (0, j)),
        out_shape=jax.ShapeDtypeStruct((n, 6 * D_MODEL), F32),
        compiler_params=_cparams(("arbitrary",)),
        name="adaln_mod",
    )(c, w_ada, b_ada.reshape(1, 6 * D_MODEL))


def _inproj_kernel(x_ref, shift_ref, scale_ref, g_ref, wu_ref, wq_ref, wkt_ref, wv_ref, wg_ref,
                   cosq_ref, sina_ref, sinb_ref, cost_ref, sint_ref,
                   u_ref, q_ref, kt_ref, v_ref, gate_ref):
    x = x_ref[...]
    h = _rms(x, g_ref[...]) * (1.0 + scale_ref[...]) + shift_ref[...]
    hb = h.astype(BF16)
    u_ref[...] = _dot(hb, wu_ref[...])
    v = _dot(hb, wv_ref[...])
    for hd in range(N_HEADS):
        v_ref[pl.ds(hd, x.shape[0], stride=N_HEADS), :] = v[:, hd * HEAD_W:(hd + 1) * HEAD_W]
    gate_ref[...] = jax.nn.sigmoid(_dot(hb, wg_ref[...])).astype(BF16)
    q = _dot(hb, wq_ref[...])
    blocks = [q[:, j:j + LANE] for j in range(0, ATTN_W, LANE)]
    q_up = jnp.concatenate([pltpu.roll(b, LANE - ROT_HALF, 1) for b in blocks], axis=1)
    q_dn = jnp.concatenate([pltpu.roll(b, ROT_HALF, 1) for b in blocks], axis=1)
    q = q * cosq_ref[...] + q_up * sina_ref[...] + q_dn * sinb_ref[...]
    q_ref[...] = (q * (HEAD_DIM ** -0.5 * LOG2_E)).astype(BF16)
    kt = lax.dot_general(wkt_ref[...], hb, (((1,), (1,)), ((), ())), preferred_element_type=F32)
    cos_t = cost_ref[...]
    sin_t = sint_ref[...]
    for hh in range(2 * N_HEADS):
        r0 = hh * HEAD_DIM
        x1 = kt[r0:r0 + ROT_HALF, :]
        x2 = kt[r0 + ROT_HALF:r0 + ROT_DIM, :]
        kt_ref[r0:r0 + ROT_HALF, :] = x1 * cos_t - x2 * sin_t
        kt_ref[r0 + ROT_HALF:r0 + ROT_DIM, :] = x2 * cos_t + x1 * sin_t
        kt_ref[r0 + ROT_DIM:r0 + HEAD_DIM, :] = kt[r0 + ROT_DIM:r0 + HEAD_DIM, :]


def _rope_tables(pos):
    inv_freq = ROPE_THETA ** (-jnp.arange(ROT_HALF, dtype=F32) * (2.0 / ROT_DIM))
    ang = pos.astype(F32)[:, None] * inv_freq[None, :]
    cos, sin = jnp.cos(ang), jnp.sin(ang)
    s = pos.shape[0]
    pad = HEAD_DIM - ROT_DIM
    cos_h = jnp.concatenate([cos, cos, jnp.ones((s, pad), F32)], axis=1)
    sina_h = jnp.concatenate([-sin, jnp.zeros((s, ROT_HALF + pad), F32)], axis=1)
    sinb_h = jnp.concatenate([jnp.zeros((s, ROT_HALF), F32), sin, jnp.zeros((s, pad), F32)], axis=1)
    reps = ATTN_W // HEAD_DIM
    return (jnp.tile(cos_h, (1, reps)), jnp.tile(sina_h, (1, reps)), jnp.tile(sinb_h, (1, reps)),
            cos.T, sin.T)


def _in_projection(x, shift, scale, g, weights, tables, ts, per_token_mod):
    nb, s, _ = x.shape
    ns = s // ts
    wu, wq, wkt, wv, wg = weights
    cosq, sina, sinb, cost, sint = tables
    if per_token_mod:
        mod_spec = pl.BlockSpec((ts, D_MODEL), lambda b, i: (b * ns + i, 0))
    else:
        mod_spec = pl.BlockSpec((None, 1, D_MODEL), lambda b, i: (b, 0, 0))
    const = lambda shape: pl.BlockSpec(shape, lambda b, i: (0,) * len(shape))
    tab = pl.BlockSpec((ts, ATTN_W), lambda b, i: (i, 0))
    tab_t = pl.BlockSpec((ROT_HALF, ts), lambda b, i: (0, i))
    return pl.pallas_call(
        _inproj_kernel,
        grid=(nb, ns),
        in_specs=[pl.BlockSpec((None, ts, D_MODEL), lambda b, i: (b, i, 0)), mod_spec, mod_spec,
                  const((1, D_MODEL)),
                  const((D_MODEL, SSM_WIDTH)), const((D_MODEL, ATTN_W)), const((ATTN_W, D_MODEL)),
                  const((D_MODEL, ATTN_W)), const((D_MODEL, 2 * D_MODEL)),
                  tab, tab, tab, tab_t, tab_t],
        out_specs=[pl.BlockSpec((ts, SSM_WIDTH), lambda b, i: (i, b)),
                   pl.BlockSpec((None, ts, ATTN_W), lambda b, i: (b, i, 0)),
                   pl.BlockSpec((None, ATTN_W, ts), lambda b, i: (b, 0, i)),
                   pl.BlockSpec((None, ts * N_HEADS, HEAD_W), lambda b, i: (b, i, 0)),
                   pl.BlockSpec((None, ts, 2 * D_MODEL), lambda b, i: (b, i, 0))],
        out_shape=[jax.ShapeDtypeStruct((s, nb * SSM_WIDTH), F32),
                   jax.ShapeDtypeStruct((nb, s, ATTN_W), BF16),
                   jax.ShapeDtypeStruct((nb, ATTN_W, s), F32),
                   jax.ShapeDtypeStruct((nb, s * N_HEADS, HEAD_W), F32),
                   jax.ShapeDtypeStruct((nb, s, 2 * D_MODEL), BF16)],
        compiler_params=_cparams(("arbitrary", "arbitrary")),
        name="in_proj",
    )(x, shift, scale, g, wu, wq, wkt, wv, wg, cosq, sina, sinb, cost, sint)


def _softmax_step(s, v_bf, m_prev, l_prev, acc_prev):
    m_next = jnp.maximum(m_prev, jnp.max(s, axis=1, keepdims=True))
    p = jnp.exp2(s - jnp.tile(m_next, (1, s.shape[1] // LANE)))
    alpha = jnp.exp2(m_prev - m_next)
    l_next = alpha * l_prev + jnp.sum(p, axis=1, keepdims=True)
    acc_next = alpha * acc_prev + _dot(p.astype(BF16), v_bf)
    return m_next, l_next, acc_next


def _diff_head_out(lam, acc0, l0, acc1, l1, g):
    o = acc0 / l0 - lam * (acc1 / l1)
    return _rms(o, g) * (1.0 - LAM_INIT)


def _init_softmax_state(m_ref, l_ref, acc_ref):
    m_ref[...] = jnp.full(m_ref.shape, NEG_INF, F32)
    l_ref[...] = jnp.zeros(l_ref.shape, F32)
    acc_ref[...] = jnp.zeros(acc_ref.shape, F32)


def _prompt_attn_kernel(qi_ref, ki_ref, lam_ref, q_ref, kt_ref, v_ref, g_ref, o_ref, m_ref, l_ref, acc_ref, *, t):
    pair = pl.program_id(1)
    qi = qi_ref[pair]
    ki = ki_ref[pair]

    @pl.when(ki == 0)
    def _():
        _init_softmax_state(m_ref, l_ref, acc_ref)

    def update(masked):
        if masked:
            visible = (lax.broadcasted_iota(jnp.int32, (t, t), 1) <= lax.broadcasted_iota(jnp.int32, (t, t), 0))
        for h in range(N_HEADS):
            v_bf = v_ref[pl.ds(h, t, stride=N_HEADS), :].astype(BF16)
            for c in range(2):
                hc = 2 * h + c
                r0 = hc * HEAD_DIM
                s = _dot(q_ref[:, r0:r0 + HEAD_DIM], kt_ref[r0:r0 + HEAD_DIM, :].astype(BF16))
                if masked:
                    s = jnp.where(visible, s, NEG_INF)
                m_ref[hc], l_ref[hc], acc_ref[hc] = _softmax_step(s, v_bf, m_ref[hc], l_ref[hc], acc_ref[hc])

    @pl.when(ki < qi)
    def _():
        update(False)

    @pl.when(ki == qi)
    def _():
        update(True)
        lam = lam_ref[0]
        g = g_ref[...]
        for h in range(N_HEADS):
            o_ref[:, h * HEAD_W:(h + 1) * HEAD_W] = _diff_head_out(
                lam, acc_ref[2 * h], l_ref[2 * h], acc_ref[2 * h + 1], l_ref[2 * h + 1], g)


def _prompt_attention(lam, q, kt, v, subln_g, t):
    b, s, _ = q.shape
    n = s // t
    pairs = [(qi, ki) for qi in range(n) for ki in range(qi + 1)]
    qi_tab = jnp.asarray([p[0] for p in pairs], jnp.int32)
    ki_tab = jnp.asarray([p[1] for p in pairs], jnp.int32)
    kern = functools.partial(_prompt_attn_kernel, t=t)
    grid_spec = pltpu.PrefetchScalarGridSpec(
        num_scalar_prefetch=3,
        grid=(b, len(pairs)),
        in_specs=[pl.BlockSpec((None, t, ATTN_W), lambda bi, p, qt, kt_, lam_: (bi, qt[p], 0)),
                  pl.BlockSpec((None, ATTN_W, t), lambda bi, p, qt, kt_, lam_: (bi, 0, kt_[p])),
                  pl.BlockSpec((None, t * N_HEADS, HEAD_W), lambda bi, p, qt, kt_, lam_: (bi, kt_[p], 0)),
                  pl.BlockSpec((1, HEAD_W), lambda bi, p, qt, kt_, lam_: (0, 0))],
        out_specs=pl.BlockSpec((None, t, ATTN_W), lambda bi, p, qt, kt_, lam_: (bi, qt[p], 0)),
        scratch_shapes=[pltpu.VMEM((2 * N_HEADS, t, LANE), F32), pltpu.VMEM((2 * N_HEADS, t, LANE), F32),
                        pltpu.VMEM((2 * N_HEADS, t, HEAD_W), F32)])
    return pl.pallas_call(
        kern, grid_spec=grid_spec,
        out_shape=jax.ShapeDtypeStruct((b, s, ATTN_W), F32),
        compiler_params=_cparams(("arbitrary", "arbitrary")),
        name="prompt_attn",
    )(qi_tab, ki_tab, lam, q, kt, v, subln_g)


def _sample_attn_kernel(pt_ref, lam_ref, q_ref, ktn_ref, vn_ref, g_ref, *rest, pages, page):
    k_refs = rest[:pages]
    v_refs = rest[pages:2 * pages]
    o_ref, kcat_ref, vcat_ref, m_ref, l_ref, acc_ref = rest[2 * pages:]
    j = pl.program_id(1)
    nq = q_ref.shape[0]
    hrows = 2 * nq

    @pl.when(j == 0)
    def _():
        _init_softmax_state(m_ref, l_ref, acc_ref)

    for i in range(pages):
        kcat_ref[:, i * page:(i + 1) * page] = k_refs[i][...].astype(BF16)
        for h in range(N_HEADS):
            vcat_ref[h, i * page:(i + 1) * page, :] = v_refs[i][pl.ds(h, page, stride=N_HEADS), :].astype(BF16)

    q = q_ref[...]
    q_parts = [q[:, hc * HEAD_DIM:(hc + 1) * HEAD_DIM] for hc in range(2 * N_HEADS)]
    s = jnp.concatenate([_dot(q_parts[hc], kcat_ref[hc * HEAD_DIM:(hc + 1) * HEAD_DIM, :])
                         for hc in range(2 * N_HEADS)], axis=0)
    m_prev = m_ref[...]
    m_next = jnp.maximum(m_prev, jnp.max(s, axis=1, keepdims=True))
    p = jnp.exp2(s - jnp.tile(m_next, (1, s.shape[1] // LANE)))
    alpha = jnp.exp2(m_prev - m_next)
    l_ref[...] = alpha * l_ref[...] + jnp.sum(p, axis=1, keepdims=True)
    pb = p.astype(BF16)
    pv = jnp.concatenate([_dot(pb[h * hrows:(h + 1) * hrows], vcat_ref[h]) for h in range(N_HEADS)], axis=0)
    acc_ref[...] = alpha * acc_ref[...] + pv
    m_ref[...] = m_next

    @pl.when(j == pl.num_programs(1) - 1)
    def _():
        rows = 2 * N_HEADS * nq
        s_new = jnp.concatenate([_dot(q_parts[hc], ktn_ref[hc * HEAD_DIM:(hc + 1) * HEAD_DIM, :].astype(BF16))
                                 for hc in range(2 * N_HEADS)], axis=0)
        qpos = lax.broadcasted_iota(jnp.int32, (rows, nq), 0) % nq
        kpos = lax.broadcasted_iota(jnp.int32, (rows, nq), 1)
        s_new = jnp.where(kpos <= qpos, s_new, NEG_INF)
        m_old = m_ref[...]
        m_fin = jnp.maximum(m_old, jnp.max(s_new, axis=1, keepdims=True))
        p_new = jnp.exp2(s_new - m_fin[:, :nq])
        a_fin = jnp.exp2(m_old - m_fin)
        l_fin = a_fin * l_ref[...] + jnp.sum(p_new, axis=1, keepdims=True)
        pnb = p_new.astype(BF16)
        pv_new = jnp.concatenate(
            [_dot(pnb[h * hrows:(h + 1) * hrows], vn_ref[:, h * HEAD_W:(h + 1) * HEAD_W].astype(BF16))
             for h in range(N_HEADS)], axis=0)
        acc_fin = a_fin * acc_ref[...] + pv_new
        lam = lam_ref[0]
        g = g_ref[...]
        for h in range(N_HEADS):
            r0 = h * hrows
            o_ref[:, h * HEAD_W:(h + 1) * HEAD_W] = _diff_head_out(
                lam, acc_fin[r0:r0 + nq], l_fin[r0:r0 + nq], acc_fin[r0 + nq:r0 + hrows], l_fin[r0 + nq:r0 + hrows], g)


def _sample_attention(page_table, lam, q, kt_new, v_new, subln_g, cache_kt, cache_v2, pages):
    b, nq, _ = q.shape
    n_pages = page_table.shape[1]
    page = cache_kt.shape[2]
    nj = n_pages // pages
    rows = 2 * N_HEADS * nq
    kern = functools.partial(_sample_attn_kernel, pages=pages, page=page)

    def k_spec(i):
        return pl.BlockSpec((None, ATTN_W, page), lambda bi, j, pt, lam_: (pt[bi, j * pages + i], 0, 0))

    def v_spec(i):
        return pl.BlockSpec((None, page * N_HEADS, HEAD_W), lambda bi, j, pt, lam_: (pt[bi, j * pages + i], 0, 0))

    grid_spec = pltpu.PrefetchScalarGridSpec(
        num_scalar_prefetch=2,
        grid=(b, nj),
        in_specs=[pl.BlockSpec((None, nq, ATTN_W), lambda bi, j, pt, lam_: (bi, 0, 0)),
                  pl.BlockSpec((None, ATTN_W, nq), lambda bi, j, pt, lam_: (bi, 0, 0)),
                  pl.BlockSpec((None, nq, ATTN_W), lambda bi, j, pt, lam_: (bi, 0, 0)),
                  pl.BlockSpec((1, HEAD_W), lambda bi, j, pt, lam_: (0, 0))]
                 + [k_spec(i) for i in range(pages)] + [v_spec(i) for i in range(pages)],
        out_specs=pl.BlockSpec((None, nq, ATTN_W), lambda bi, j, pt, lam_: (bi, 0, 0)),
        scratch_shapes=[pltpu.VMEM((ATTN_W, pages * page), BF16),
                        pltpu.VMEM((N_HEADS, pages * page, HEAD_W), BF16),
                        pltpu.VMEM((rows, LANE), F32), pltpu.VMEM((rows, LANE), F32),
                        pltpu.VMEM((rows, HEAD_W), F32)])
    return pl.pallas_call(
        kern, grid_spec=grid_spec,
        out_shape=jax.ShapeDtypeStruct((b, nq, ATTN_W), F32),
        compiler_params=_cparams(("arbitrary", "arbitrary")),
        name="sample_attn",
    )(page_table, lam, q, kt_new, v_new, subln_g, *([cache_kt] * pages), *([cache_v2] * pages))


def _ssm_kernel(u_ref, h0r_ref, h0i_ref, ar_ref, ai_ref, bbr_ref, bbi_ref, ccr_ref, cci_ref, d_ref,
                wglu_ref, bglu_ref, y_ref, hr_out, hi_out, sr_ref, si_ref, br_ref, bi_ref, tm_ref,
                *, steps, nb, lane_chunk, batch_on_lanes):
    i = pl.program_id(0)

    @pl.when(i == 0)
    def _():
        sr_ref[...] = h0r_ref[...]
        si_ref[...] = h0i_ref[...]

    n_slabs = SSM_WIDTH // LANE
    if batch_on_lanes:
        for b in range(nb):
            for c in range(n_slabs):
                col = b * SSM_WIDTH + c * LANE
                tm_ref[c, pl.ds(b, steps, stride=nb), :] = u_ref[:, col:col + LANE]
        u = jnp.concatenate([tm_ref[c] for c in range(n_slabs)], axis=1)
    else:
        u = u_ref[...]
    ub = u.astype(BF16)
    for m in range(SSM_WIDTH // SSM_TILE_IN):
        cin = slice(m * SSM_TILE_IN, (m + 1) * SSM_TILE_IN)
        cst = slice(m * SSM_TILE_STATE, (m + 1) * SSM_TILE_STATE)
        br_ref[:, cst] = _dot(ub[:, cin], bbr_ref[cin, cst])
        bi_ref[:, cst] = _dot(ub[:, cin], bbi_ref[cin, cst])

    for c0 in range(0, SSM_LANES, lane_chunk):
        lanes = pl.ds(c0, lane_chunk)
        a_r = jnp.broadcast_to(ar_ref[:, lanes], (nb, lane_chunk))
        a_i = jnp.broadcast_to(ai_ref[:, lanes], (nb, lane_chunk))

        def step(t, carry):
            h_r, h_i = carry
            rows = pl.ds(pl.multiple_of(t * nb, nb), nb)
            n_r = a_r * h_r - a_i * h_i + br_ref[rows, lanes]
            n_i = a_r * h_i + a_i * h_r + bi_ref[rows, lanes]
            br_ref[rows, lanes] = n_r
            bi_ref[rows, lanes] = n_i
            return n_r, n_i

        h_r, h_i = lax.fori_loop(0, steps, step, (sr_ref[:, lanes], si_ref[:, lanes]), unroll=4)
        sr_ref[:, lanes] = h_r
        si_ref[:, lanes] = h_i

    y_parts = []
    for m in range(SSM_WIDTH // SSM_TILE_IN):
        cin = slice(m * SSM_TILE_IN, (m + 1) * SSM_TILE_IN)
        cst = slice(m * SSM_TILE_STATE, (m + 1) * SSM_TILE_STATE)
        y_parts.append(_dot(br_ref[:, cst].astype(BF16), ccr_ref[cst, cin])
                       - _dot(bi_ref[:, cst].astype(BF16), cci_ref[cst, cin]))
    y = jnp.concatenate(y_parts, axis=1) + d_ref[...] * u
    y = jax.nn.gelu(y, approximate=True)
    y = y * jax.nn.sigmoid(_dot(y.astype(BF16), wglu_ref[...]) + bglu_ref[...])
    if batch_on_lanes:
        for c in range(n_slabs):
            tm_ref[c] = y[:, c * LANE:(c + 1) * LANE]
        for b in range(nb):
            for c in range(n_slabs):
                col = b * SSM_WIDTH + c * LANE
                y_ref[:, col:col + LANE] = tm_ref[c, pl.ds(b, steps, stride=nb), :]
    else:
        y_ref[...] = y

    @pl.when(i == pl.num_programs(0) - 1)
    def _():
        hr_out[...] = sr_ref[...]
        hi_out[...] = si_ref[...]


def _ssm_params(a_re, a_im, log_dt, b_re, b_im, c_re, c_im, d_skip):
    lam_re = jnp.minimum(a_re.astype(F32), -MIN_DECAY)
    lam_im = a_im.astype(F32)
    dt = jnp.exp(log_dt.astype(F32))[:, None]
    mag = jnp.exp(lam_re * dt)
    ab_re = mag * jnp.cos(lam_im * dt)
    ab_im = mag * jnp.sin(lam_im * dt)
    den = lam_re * lam_re + lam_im * lam_im
    n_re = ab_re - 1.0
    n_im = ab_im
    zoh_re = (n_re * lam_re + n_im * lam_im) / den
    zoh_im = (n_im * lam_re - n_re * lam_im) / den
    br, bi = b_re.astype(F32), b_im.astype(F32)
    bb_re = zoh_re[..., None] * br - zoh_im[..., None] * bi
    bb_im = zoh_re[..., None] * bi + zoh_im[..., None] * br
    eye = jnp.eye(SSM_GROUPS, dtype=F32)

    def in_blockdiag(bb):
        m = jnp.einsum('gpc,gh->gchp', bb, eye)
        return m.reshape(SSM_WIDTH, SSM_LANES).astype(BF16)

    def out_blockdiag(cc):
        m = jnp.einsum('gcp,gh->gphc', cc.astype(F32), eye)
        return m.reshape(SSM_LANES, SSM_WIDTH).astype(BF16)

    return (ab_re.reshape(1, SSM_LANES), ab_im.reshape(1, SSM_LANES), in_blockdiag(bb_re), in_blockdiag(bb_im),
            out_blockdiag(c_re), out_blockdiag(c_im), d_skip.astype(F32).reshape(1, SSM_WIDTH))


def _ssm_scan(u, h0_re, h0_im, params, w_glu, b_glu, nb, steps, batch_on_lanes):
    rows = steps * nb
    n_chunks = u.shape[0] * u.shape[1] // (rows * SSM_WIDTH)
    ab_re, ab_im, bbr, bbi, ccr, cci, d_row = params
    lane_chunk = min(SSM_LANES, 8 * 1024 // nb)
    kern = functools.partial(_ssm_kernel, steps=steps, nb=nb, lane_chunk=lane_chunk, batch_on_lanes=batch_on_lanes)
    const = lambda shape: pl.BlockSpec(shape, lambda i: (0,) * len(shape))
    io_block = (steps, nb * SSM_WIDTH) if batch_on_lanes else (rows, SSM_WIDTH)
    return pl.pallas_call(
        kern, grid=(n_chunks,),
        in_specs=[pl.BlockSpec(io_block, lambda i: (i, 0)),
                  const((nb, SSM_LANES)), const((nb, SSM_LANES)),
                  const((1, SSM_LANES)), const((1, SSM_LANES)),
                  const((SSM_WIDTH, SSM_LANES)), const((SSM_WIDTH, SSM_LANES)),
                  const((SSM_LANES, SSM_WIDTH)), const((SSM_LANES, SSM_WIDTH)),
                  const((1, SSM_WIDTH)), const((SSM_WIDTH, SSM_WIDTH)), const((1, SSM_WIDTH))],
        out_specs=[pl.BlockSpec(io_block, lambda i: (i, 0)),
                   const((nb, SSM_LANES)), const((nb, SSM_LANES))],
        out_shape=[jax.ShapeDtypeStruct(u.shape, F32),
                   jax.ShapeDtypeStruct((nb, SSM_LANES), F32), jax.ShapeDtypeStruct((nb, SSM_LANES), F32)],
        scratch_shapes=[pltpu.VMEM((nb, SSM_LANES), F32), pltpu.VMEM((nb, SSM_LANES), F32),
                        pltpu.VMEM((rows, SSM_LANES), F32), pltpu.VMEM((rows, SSM_LANES), F32),
                        pltpu.VMEM((SSM_WIDTH // LANE, rows, LANE), F32)],
        compiler_params=_cparams(("arbitrary",)),
        name="ssm_scan",
    )(u, h0_re, h0_im, ab_re, ab_im, bbr, bbi, ccr, cci, d_row, w_glu, b_glu)


def _post_kernel(yssm_ref, yattn_ref, gate_ref, x_ref, gate1_ref, shift2_ref, scale2_ref,
                 pssm_ref, pattn_ref, wo_ref, g2_ref, wrh_ref, wrl_ref, br_ref, h2_in_ref,
                 x1_ref, h2_ref, ridx_ref, rgate_ref, cnt_ref, carry_ref, *, ts):
    del h2_in_ref

    @pl.when(jnp.logical_and(pl.program_id(0) == 0, pl.program_id(1) == 0))
    def _():
        carry_ref[...] = jnp.zeros(carry_ref.shape, F32)

    gates = gate_ref[...]
    a = _dot(yssm_ref[...].astype(BF16), pssm_ref[...])
    b = _dot(yattn_ref[...].astype(BF16), pattn_ref[...])
    merged = gates[:, :D_MODEL] * a + gates[:, D_MODEL:] * b
    x1 = x_ref[...] + gate1_ref[...] * _dot(merged.astype(BF16), wo_ref[...])
    x1_ref[...] = x1
    h2 = _rms(x1, g2_ref[...]) * (1.0 + scale2_ref[...]) + shift2_ref[...]
    _store_tile_rows(h2_ref, h2)

    h_hi = h2.astype(BF16)
    h_lo = (h2 - h_hi.astype(F32)).astype(BF16)
    nt = (((1,), (1,)), ((), ()))
    w_hi = wrh_ref[...]
    logits = (lax.dot_general(w_hi, h_hi, nt, preferred_element_type=F32)
              + lax.dot_general(w_hi, h_lo, nt, preferred_element_type=F32)
              + lax.dot_general(wrl_ref[...], h_hi, nt, preferred_element_type=F32)) + br_ref[...]

    expert = lax.broadcasted_iota(jnp.int32, (N_EXPERTS, ts), 0)
    work = logits
    idxs, vals = [], []
    for _ in range(TOP_K):
        mk = jnp.max(work, axis=0, keepdims=True)
        ik = jnp.min(jnp.where(work == mk, expert, N_EXPERTS), axis=0, keepdims=True)
        idxs.append(ik)
        vals.append(mk)
        work = jnp.where(expert == ik, -jnp.inf, work)
    exps = [jnp.exp(v - vals[0]) for v in vals]
    den = exps[0] + exps[1] + exps[2] + exps[3]
    rgate_ref[...] = jnp.concatenate([e / den for e in exps], axis=0)

    chosen = jnp.zeros((N_EXPERTS, ts), F32)
    for ik in idxs:
        chosen = chosen + jnp.where(expert == ik, 1.0, 0.0)
    earlier = jnp.where(lax.broadcasted_iota(jnp.int32, (ts, ts), 0) < lax.broadcasted_iota(jnp.int32, (ts, ts), 1),
                        1.0, 0.0).astype(BF16)
    before = _dot(chosen.astype(BF16), earlier) + carry_ref[...]
    ranks = [jnp.sum(jnp.where(expert == ik, before, 0.0), axis=0, keepdims=True).astype(jnp.int32) for ik in idxs]
    ridx_ref[...] = jnp.concatenate(idxs + ranks, axis=0)
    carry_ref[...] = carry_ref[...] + jnp.sum(chosen, axis=1, keepdims=True)
    cnt_ref[...] = carry_ref[...]


def _post_mixer(yssm_tm, yattn, gate, x, gate1, shift2, scale2, weights, ts, per_token_mod,
                h2_tokens, h2_token0, h2_prev):
    nb, s, _ = x.shape
    ns = s // ts
    h2_block0 = h2_token0 // ts
    assert h2_token0 % ts == 0
    p_ssm, p_attn, w_o, g2, wr_hi, wr_lo, b_r = weights
    if per_token_mod:
        mod_spec = pl.BlockSpec((ts, D_MODEL), lambda b, i: (b * ns + i, 0))
    else:
        mod_spec = pl.BlockSpec((None, 1, D_MODEL), lambda b, i: (b, 0, 0))
    const = lambda shape: pl.BlockSpec(shape, lambda b, i: (0,) * len(shape))
    tok = lambda w: pl.BlockSpec((None, ts, w), lambda b, i: (b, i, 0))
    tok_t = lambda r: pl.BlockSpec((None, r, ts), lambda b, i: (b, 0, i))
    in_specs = [pl.BlockSpec((ts, SSM_WIDTH), lambda b, i: (i, b)), tok(ATTN_W), tok(2 * D_MODEL), tok(D_MODEL),
                mod_spec, mod_spec, mod_spec,
                const((SSM_WIDTH, D_MODEL)), const((ATTN_W, D_MODEL)), const((D_MODEL, D_MODEL)),
                const((1, D_MODEL)), const((N_EXPERTS, D_MODEL)), const((N_EXPERTS, D_MODEL)),
                const((N_EXPERTS, 1)), pl.BlockSpec(memory_space=pl.ANY)]
    operands = [yssm_tm, yattn, gate, x, gate1, shift2, scale2, p_ssm, p_attn, w_o, g2, wr_hi, wr_lo, b_r, h2_prev]
    aliases = {len(operands) - 1: 1}
    return pl.pallas_call(
        functools.partial(_post_kernel, ts=ts), grid=(nb, ns),
        in_specs=in_specs,
        out_specs=[tok(D_MODEL),
                   pl.BlockSpec((ts * ROW_CHUNKS, LANE), lambda b, i: (h2_block0 + b * ns + i, 0)),
                   tok_t(2 * TOP_K), tok_t(TOP_K), const((N_EXPERTS, 1))],
        out_shape=[jax.ShapeDtypeStruct((nb, s, D_MODEL), F32),
                   jax.ShapeDtypeStruct((h2_tokens * ROW_CHUNKS, LANE), F32),
                   jax.ShapeDtypeStruct((nb, 2 * TOP_K, s), jnp.int32),
                   jax.ShapeDtypeStruct((nb, TOP_K, s), F32),
                   jax.ShapeDtypeStruct((N_EXPERTS, 1), F32)],
        scratch_shapes=[pltpu.VMEM((N_EXPERTS, 1), F32)],
        input_output_aliases=aliases,
        compiler_params=_cparams(("arbitrary", "arbitrary")),
        name="post_mixer",
    )(*operands)


def _store_tile_rows(ref, x, row0=0):
    n = x.shape[0]
    for j in range(ROW_CHUNKS):
        ref[pl.ds(row0 * ROW_CHUNKS + j, n, stride=ROW_CHUNKS), :] = x[:, j * LANE:(j + 1) * LANE]


def _load_tile_rows(ref, n, row0=0):
    return jnp.concatenate([ref[pl.ds(row0 * ROW_CHUNKS + j, n, stride=ROW_CHUNKS), :]
                            for j in range(ROW_CHUNKS)], axis=1)


def _row_copy(src_hbm, src_start, dst_ref, dst_row, sem):
    return pltpu.make_async_copy(src_hbm.at[pl.ds(src_start, ROW_CHUNKS)],
                                 dst_ref.at[pl.ds(dst_row * ROW_CHUNKS, ROW_CHUNKS)], sem)


def _start_rows(src_hbm, idx_ref, dst_ref, sem, n):
    for r in range(n):
        src_start = pl.multiple_of(idx_ref[0, r] * ROW_CHUNKS, ROW_CHUNKS)
        _row_copy(src_hbm, src_start, dst_ref, r, sem).start(priority=r % 2)


def _wait_rows(src_hbm, dst_ref, sem, n):
    for r in range(n):
        _row_copy(src_hbm, 0, dst_ref, r, sem).wait()


def _moe_kernel(iblk_ref, iexp_ref, ilo_ref, ihi_ref, tok_ref, tok_next_ref, tok_next2_ref, h_hbm, wup_ref, bup_ref,
                wdn_ref, bdn_ref, o_ref, xbuf0_ref, xbuf1_ref, xbuf2_ref, merged_ref, wupb_ref, wdnb_ref, sem_ref,
                *, blk):
    w = pl.program_id(0)
    last = pl.num_programs(0) - 1

    @pl.when(w == 0)
    def _():
        _start_rows(h_hbm, tok_ref, xbuf0_ref, sem_ref.at[0], blk)
        _start_rows(h_hbm, tok_next_ref, xbuf1_ref, sem_ref.at[1], blk)
        merged_ref[...] = jnp.zeros(merged_ref.shape, F32)

    @pl.when(jnp.logical_or(w == 0, iexp_ref[w] != iexp_ref[jnp.maximum(w - 1, 0)]))
    def _():
        wupb_ref[...] = wup_ref[...].astype(BF16)
        wdnb_ref[...] = wdn_ref[...].astype(BF16)

    def step(cur, ahead1, ahead2):
        bufs = (xbuf0_ref, xbuf1_ref, xbuf2_ref)
        _wait_rows(h_hbm, bufs[cur], sem_ref.at[cur], blk)
        _start_rows(h_hbm, tok_next2_ref, bufs[ahead2], sem_ref.at[ahead2], blk)
        xb = _load_tile_rows(bufs[cur], blk).astype(BF16)
        gu = _dot(xb, wupb_ref[...]) + bup_ref[...]
        g_lin = jnp.minimum(gu[:, :D_FF], SWIGLU_LIMIT)
        u_lin = jnp.clip(gu[:, D_FF:], -SWIGLU_LIMIT, SWIGLU_LIMIT)
        act = g_lin * jax.nn.sigmoid(SWIGLU_ALPHA * g_lin) * (u_lin + 1.0)
        res = _dot(act.astype(BF16), wdnb_ref[...]) + bdn_ref[...]

        rows = iblk_ref[w] * blk + lax.broadcasted_iota(jnp.int32, (blk, 1), 0)
        mine = jnp.logical_and(rows >= ilo_ref[w], rows < ihi_ref[w])
        merged = jnp.where(mine, res, merged_ref[...])
        merged_ref[...] = merged
        _store_tile_rows(o_ref, merged)

        @pl.when(w == last)
        def _():
            _wait_rows(h_hbm, bufs[ahead1], sem_ref.at[ahead1], blk)
            _wait_rows(h_hbm, bufs[ahead2], sem_ref.at[ahead2], blk)

    for phase in range(3):
        @pl.when(w % 3 == phase)
        def _(phase=phase):
            step(phase, (phase + 1) % 3, (phase + 2) % 3)


def _routed_experts(h2, row_tok, plan, w_up, b_up, w_down, b_down, blk):
    item_blk, item_exp, item_lo, item_hi = plan
    n_rows = row_tok.shape[0]
    n_blocks = n_rows // blk
    n_items = item_blk.shape[0]
    tok3 = row_tok.reshape(n_blocks, 1, blk)
    kern = functools.partial(_moe_kernel, blk=blk)
    last = n_items - 1

    def tok_spec(ahead):
        return pl.BlockSpec((None, 1, blk), lambda w, ib, ie, lo, hi: (ib[jnp.minimum(w + ahead, last)], 0, 0),
                            memory_space=pltpu.SMEM)

    grid_spec = pltpu.PrefetchScalarGridSpec(
        num_scalar_prefetch=4,
        grid=(n_items,),
        in_specs=[tok_spec(0), tok_spec(1), tok_spec(2),
                  pl.BlockSpec(memory_space=pl.ANY),
                  pl.BlockSpec((None, D_MODEL, 2 * D_FF), lambda w, ib, ie, lo, hi: (ie[w], 0, 0)),
                  pl.BlockSpec((None, 1, 2 * D_FF), lambda w, ib, ie, lo, hi: (ie[w], 0, 0)),
                  pl.BlockSpec((None, D_FF, D_MODEL), lambda w, ib, ie, lo, hi: (ie[w], 0, 0)),
                  pl.BlockSpec((None, 1, D_MODEL), lambda w, ib, ie, lo, hi: (ie[w], 0, 0))],
        out_specs=pl.BlockSpec((blk * ROW_CHUNKS, LANE), lambda w, ib, ie, lo, hi: (ib[w], 0)),
        scratch_shapes=[pltpu.VMEM((blk * ROW_CHUNKS, LANE), F32)] * 3
                       + [pltpu.VMEM((blk, D_MODEL), F32),
                          pltpu.VMEM((D_MODEL, 2 * D_FF), BF16), pltpu.VMEM((D_FF, D_MODEL), BF16),
                          pltpu.SemaphoreType.DMA((3,))])
    return pl.pallas_call(
        kern, grid_spec=grid_spec,
        out_shape=jax.ShapeDtypeStruct((n_rows * ROW_CHUNKS, LANE), F32),
        compiler_params=_cparams(("arbitrary",)),
        name="routed_experts",
    )(item_blk, item_exp, item_lo, item_hi, tok3, tok3, tok3, h2, w_up,
      b_up.reshape(N_EXPERTS, 1, 2 * D_FF), w_down, b_down.reshape(N_EXPERTS, 1, D_MODEL))


def _route_plan(ridx, counts, blk):
    n_tok = ridx.shape[1]
    top_i, rank = ridx[:TOP_K], ridx[TOP_K:]
    counts = counts.reshape(N_EXPERTS).astype(jnp.int32)
    ends = jnp.cumsum(counts)
    starts = ends - counts
    experts = jnp.arange(N_EXPERTS, dtype=jnp.int32)
    start_of = jnp.sum(jnp.where(top_i[..., None] == experts, starts, 0), axis=-1)
    pos = (start_of + rank).astype(jnp.int32)
    t_flat = jnp.tile(jnp.arange(n_tok, dtype=jnp.int32), TOP_K)
    _, row_tok = lax.sort_key_val(pos.reshape(-1), t_flat)
    n_blocks = n_tok * TOP_K // blk
    n_items = n_blocks + N_EXPERTS - 1
    first_blk = starts // blk
    n_blk = jnp.where(counts > 0, (ends - 1) // blk - first_blk + 1, 0)
    item_end = jnp.cumsum(n_blk)
    item_start = item_end - n_blk
    w = jnp.arange(n_items, dtype=jnp.int32)
    used = w < item_end[-1]
    e_w = jnp.minimum(jnp.sum((item_end[None, :] <= w[:, None]).astype(jnp.int32), axis=1), N_EXPERTS - 1)
    e_last = jnp.max(jnp.where(counts > 0, experts, 0))
    sel = e_w[:, None] == experts[None, :]
    of_item = lambda per_expert: jnp.sum(jnp.where(sel, per_expert[None, :], 0), axis=1)
    item_blk = jnp.where(used, of_item(first_blk) + w - of_item(item_start), n_blocks - 1).astype(jnp.int32)
    item_exp = jnp.where(used, e_w, e_last).astype(jnp.int32)
    item_lo = jnp.where(used, of_item(starts), 0).astype(jnp.int32)
    item_hi = jnp.where(used, of_item(ends), 0).astype(jnp.int32)
    return pos, row_tok, (item_blk, item_exp, item_lo, item_hi)


def _final_kernel(pos_ref, pos_next_ref, x1_ref, gate2_ref, rgate_ref, g_ref, rows_hbm, o_ref,
                  gbuf0_ref, gbuf1_ref, sem_ref, *, ts):
    i = pl.program_id(0)
    last = pl.num_programs(0) - 1
    n_rows = ts * TOP_K

    @pl.when(i == 0)
    def _():
        _start_rows(rows_hbm, pos_ref, gbuf0_ref, sem_ref.at[0], n_rows)

    def step(cur_ref, cur_sem, nxt_ref, nxt_sem):
        _wait_rows(rows_hbm, cur_ref, cur_sem, n_rows)
        _start_rows(rows_hbm, pos_next_ref, nxt_ref, nxt_sem, n_rows)
        rg = rgate_ref[...]
        ffn = rg[:, 0:1] * _load_tile_rows(cur_ref, ts)
        for k in range(1, TOP_K):
            ffn = ffn + rg[:, k:k + 1] * _load_tile_rows(cur_ref, ts, row0=k * ts)
        x = x1_ref[...] + gate2_ref[...] * ffn
        o_ref[...] = _rms(x, g_ref[...])

        @pl.when(i == last)
        def _():
            _wait_rows(rows_hbm, nxt_ref, nxt_sem, n_rows)

    @pl.when(i % 2 == 0)
    def _():
        step(gbuf0_ref, sem_ref.at[0], gbuf1_ref, sem_ref.at[1])

    @pl.when(i % 2 == 1)
    def _():
        step(gbuf1_ref, sem_ref.at[1], gbuf0_ref, sem_ref.at[0])


def _combine_final(x1, gate2, rgate, final_g, rows, pos, ts, per_token_mod):
    nb, s, _ = x1.shape
    ns = s // ts
    n_tiles = nb * ns
    pos3 = pos.reshape(TOP_K, n_tiles, ts).transpose(1, 0, 2).reshape(n_tiles, 1, ts * TOP_K)
    if per_token_mod:
        mod_spec = pl.BlockSpec((ts, D_MODEL), lambda i: (i, 0))
    else:
        mod_spec = pl.BlockSpec((None, 1, D_MODEL), lambda i: (i // ns, 0, 0))
    kern = functools.partial(_final_kernel, ts=ts)
    last = n_tiles - 1
    tok = lambda w: pl.BlockSpec((None, ts, w), lambda i: (i // ns, i % ns, 0))
    return pl.pallas_call(
        kern, grid=(n_tiles,),
        in_specs=[pl.BlockSpec((None, 1, ts * TOP_K), lambda i: (i, 0, 0), memory_space=pltpu.SMEM),
                  pl.BlockSpec((None, 1, ts * TOP_K), lambda i: (jnp.minimum(i + 1, last), 0, 0),
                               memory_space=pltpu.SMEM),
                  tok(D_MODEL), mod_spec, tok(TOP_K),
                  pl.BlockSpec((1, D_MODEL), lambda i: (0, 0)),
                  pl.BlockSpec(memory_space=pl.ANY)],
        out_specs=tok(D_MODEL),
        out_shape=jax.ShapeDtypeStruct((nb, s, D_MODEL), F32),
        scratch_shapes=[pltpu.VMEM((ts * TOP_K * ROW_CHUNKS, LANE), F32),
                        pltpu.VMEM((ts * TOP_K * ROW_CHUNKS, LANE), F32), pltpu.SemaphoreType.DMA((2,))],
        compiler_params=_cparams(("arbitrary",)),
        name="combine_final",
    )(pos3, pos3, x1, gate2, rgate, final_g, rows)


def _group_mixers(x, mod, pos, h0_re, h0_im, past, lam, wts, *, per_token_mod, ts, seq_major_tokens,
                  h2_tokens, h2_token0, h2_prev):
    nb, s, _ = x.shape
    shift1, scale1, gate1, shift2, scale2, _ = mod
    u_tm, q, kt, v, gate = _in_projection(x, shift1, scale1, wts['norm1_g'], wts['w_in_parts'],
                                          _rope_tables(pos), ts, per_token_mod)
    if past is None:
        y_attn = _prompt_attention(lam, q, kt, v, wts['subln_g'], t=min(512, s))
        ssm_nb, ssm_len = nb, s
        u_scan = u_tm
    else:
        n_seq, seq_len = seq_major_tokens
        page_table, cache_kt, cache_v2 = past
        q_s = q.reshape(n_seq, seq_len, ATTN_W)
        kt_s = kt.reshape(ATTN_W, n_seq, seq_len).transpose(1, 0, 2)
        v_s = v.reshape(n_seq, seq_len, ATTN_W)
        y_attn = _sample_attention(page_table, lam, q_s, kt_s, v_s, wts['subln_g'], cache_kt, cache_v2,
                                   pages=min(16, page_table.shape[1])).reshape(nb, s, ATTN_W)
        ssm_nb, ssm_len = n_seq, seq_len
        u_scan = u_tm.reshape(n_seq, seq_len, SSM_WIDTH).transpose(1, 0, 2).reshape(s, SSM_WIDTH)
    y_tm, hT_re, hT_im = _ssm_scan(u_scan, h0_re, h0_im, wts['ssm_params'], wts['w_glu'], wts['b_glu'],
                                   nb=ssm_nb, steps=min(ssm_len, max(8, 512 // ssm_nb)),
                                   batch_on_lanes=past is None)
    if past is None:
        y_ssm = y_tm
    else:
        y_ssm = y_tm.reshape(seq_len, n_seq, SSM_WIDTH).transpose(1, 0, 2).reshape(s, SSM_WIDTH)
    x1, h2, ridx, rgate, counts = _post_mixer(y_ssm, y_attn, gate, x, gate1, shift2, scale2, wts['post'], ts,
                                               per_token_mod, h2_tokens, h2_token0, h2_prev)
    return dict(x1=x1, h2=h2, ridx=ridx.transpose(1, 0, 2).reshape(2 * TOP_K, nb * s),
                rgate=rgate.transpose(0, 2, 1), counts=counts, kt=kt, v=v, h_re=hT_re, h_im=hT_im)


def _joint_route(groups, blk):
    experts = jnp.arange(N_EXPERTS, dtype=jnp.int32)
    seen = jnp.zeros((N_EXPERTS,), jnp.int32)
    ridx_all = []
    for g in groups:
        top_i, rank = g['ridx'][:TOP_K], g['ridx'][TOP_K:]
        ahead = jnp.sum(jnp.where(top_i[..., None] == experts, seen, 0), axis=-1)
        ridx_all.append(jnp.concatenate([top_i, rank + ahead], axis=0))
        seen = seen + g['counts'].reshape(N_EXPERTS).astype(jnp.int32)
    return _route_plan(jnp.concatenate(ridx_all, axis=1), seen, blk)


def kernel(x_prompt, x_sample, c_prompt, c_sample, cache_k, cache_v, state_ssm_re, state_ssm_im, page_table, norm1_g, w_ada, b_ada, w_in, ssm_a_re, ssm_a_im, ssm_log_dt, ssm_b_re, ssm_b_im, ssm_c_re, ssm_c_im, ssm_d, w_glu, b_glu, lambda_q1, lambda_k1, lambda_q2, lambda_k2, subln_g, p_ssm, p_attn, w_o, norm2_g, w_router, b_router, w_up, b_up, w_down, b_down, final_g):
    bsz_p, seq_p, _ = x_prompt.shape
    bsz_s, seq_s, _ = x_sample.shape
    n_pool, page = cache_k.shape[1], cache_k.shape[2]
    past_len = page_table.shape[1] * page
    l = 0

    w = w_in[l]
    c0 = SSM_WIDTH
    w_in_parts = (w[:, :c0].astype(BF16), w[:, c0:c0 + ATTN_W].astype(BF16),
                  w[:, c0 + ATTN_W:c0 + 2 * ATTN_W].T.astype(BF16),
                  w[:, c0 + 2 * ATTN_W:c0 + 3 * ATTN_W].astype(BF16), w[:, c0 + 3 * ATTN_W:].astype(BF16))
    wr_t = w_router[l].astype(F32).T
    wr_hi = wr_t.astype(BF16)
    wr_lo = (wr_t - wr_hi.astype(F32)).astype(BF16)
    wts = {
        'norm1_g': norm1_g[l].reshape(1, D_MODEL),
        'w_in_parts': w_in_parts,
        'subln_g': subln_g[l].reshape(1, HEAD_W),
        'ssm_params': _ssm_params(ssm_a_re[l], ssm_a_im[l], ssm_log_dt[l], ssm_b_re[l], ssm_b_im[l],
                                  ssm_c_re[l], ssm_c_im[l], ssm_d[l]),
        'w_glu': w_glu[l].astype(BF16), 'b_glu': b_glu[l].reshape(1, SSM_WIDTH),
        'post': (p_ssm[l].astype(BF16), p_attn[l].astype(BF16), w_o[l].astype(BF16),
                 norm2_g[l].reshape(1, D_MODEL), wr_hi, wr_lo, b_router[l].reshape(N_EXPERTS, 1)),
        'w_up': w_up[l], 'b_up': b_up[l], 'w_down': w_down[l], 'b_down': b_down[l],
        'final_g': final_g.reshape(1, D_MODEL),
    }
    lam = (jnp.exp(jnp.sum(lambda_q1[l].astype(F32) * lambda_k1[l].astype(F32)))
           - jnp.exp(jnp.sum(lambda_q2[l].astype(F32) * lambda_k2[l].astype(F32))) + LAM_INIT).reshape(1)

    mod = _modulation(jnp.concatenate([c_prompt, c_sample], axis=0), w_ada[l], b_ada[l])
    mod_p = [m.reshape(bsz_p, 1, D_MODEL) for m in jnp.split(mod[:bsz_p], 6, axis=-1)]
    mod_s = [jnp.repeat(m, seq_s, axis=0) for m in jnp.split(mod[bsz_p:], 6, axis=-1)]

    n_p = bsz_p * seq_p
    n_s = bsz_s * seq_s
    zeros_state = jnp.zeros((bsz_p, SSM_LANES), F32)
    h2_buffer = jnp.zeros(((n_p + n_s) * ROW_CHUNKS, LANE), F32)
    gp = _group_mixers(
        x_prompt, mod_p, jnp.arange(seq_p), zeros_state, zeros_state, None, lam, wts,
        per_token_mod=False, ts=min(256, seq_p), seq_major_tokens=None,
        h2_tokens=n_p + n_s, h2_token0=0, h2_prev=h2_buffer)

    cache_kt = jnp.transpose(cache_k[l], (0, 2, 3, 1)).reshape(n_pool, ATTN_W, page)
    cache_v2 = cache_v[l].reshape(n_pool, page * N_HEADS, HEAD_W)
    pos_s = past_len + jnp.tile(jnp.arange(seq_s), bsz_s)
    gs = _group_mixers(
        x_sample.reshape(1, n_s, D_MODEL), mod_s, pos_s,
        state_ssm_re[l].reshape(bsz_s, SSM_LANES), state_ssm_im[l].reshape(bsz_s, SSM_LANES),
        (page_table, cache_kt, cache_v2), lam, wts,
        per_token_mod=True, ts=n_s, seq_major_tokens=(bsz_s, seq_s),
        h2_tokens=n_p + n_s, h2_token0=n_p, h2_prev=gp['h2'])

    n_rows = (n_p + n_s) * TOP_K
    blk = math.gcd(256, n_rows)
    row_pos, row_tok, plan = _joint_route([gp, gs], blk)
    rows = _routed_experts(gs['h2'], row_tok, plan, wts['w_up'], wts['b_up'], wts['w_down'], wts['b_down'], blk)
    yp = _combine_final(gp['x1'], mod_p[5], gp['rgate'], wts['final_g'], rows, row_pos[:, :n_p],
                        min(128, seq_p), False)
    ys = _combine_final(gs['x1'], mod_s[5], gs['rgate'], wts['final_g'], rows, row_pos[:, n_p:],
                        min(128, n_s), True)

    k_prompt = jnp.transpose(gp['kt'].reshape(bsz_p, 2 * N_HEADS, HEAD_DIM, seq_p), (0, 3, 1, 2))[None]
    v_prompt = gp['v'].reshape(bsz_p, seq_p, N_HEADS, HEAD_W)[None]
    k_sample = jnp.transpose(gs['kt'].reshape(2 * N_HEADS, HEAD_DIM, bsz_s, seq_s), (2, 3, 0, 1))[None]
    v_sample = gs['v'].reshape(bsz_s, seq_s, N_HEADS, HEAD_W)[None]
    state = lambda h, n: h.reshape(n, SSM_GROUPS, SSM_STATE)[None]
    return (yp, ys.reshape(bsz_s, seq_s, D_MODEL), k_prompt, v_prompt,
            state(gp['h_re'], bsz_p), state(gp['h_im'], bsz_p),
            k_sample, v_sample, state(gs['h_re'], bsz_s), state(gs['h_im'], bsz_s))
```
